```python
import math
import jax, jax.numpy as jnp
from jax import lax
import numpy as np

D_MODEL = 1024
BATCH = 2
SEQ = 8192
DEPTH = 1
DEC_BATCH = 8
DEC_SEQ = 8192
PAST_LEN = 128

MIX_WIDTH = D_MODEL
HG_WIDTH = MIX_WIDTH // 2
HG_EXPAND = 128
HG_HEADS = HG_WIDTH // HG_EXPAND
HG_DV = HG_WIDTH // HG_HEADS
HG_CHUNK = 64
AT_WIDTH = MIX_WIDTH - HG_WIDTH
AT_HEAD_DIM = 64
AT_HEADS = AT_WIDTH // AT_HEAD_DIM
AT_KV_HEADS = 2
WINDOW = 128
BLOCK = 128
ROPE_THETA = 10000.0
D_FF = 2816
EPS = 1e-6
N_IN = 5 * HG_WIDTH + AT_WIDTH + 2 * AT_KV_HEADS * AT_HEAD_DIM

kernel_name = "hymba_hgrn2_window_gqa_macaron_encoder"


def rmsnorm(x, g):
    xf = x.astype(jnp.float32)
    y = xf * lax.rsqrt(jnp.mean(xf * xf, axis=-1, keepdims=True) + EPS)
    return (y * g.astype(jnp.float32)).astype(x.dtype)


def swiglu(x, wi, wo):
    gu = x @ wi.astype(x.dtype)
    gate, up = jnp.split(gu, 2, axis=-1)
    return (jax.nn.silu(gate) * up) @ wo.astype(x.dtype)


def gla_scan(q, k, v, logf):
    B, S, H, DK = q.shape
    DV = v.shape[-1]
    C = HG_CHUNK
    N = S // C

    def to_chunks(a):
        return a.reshape(B, N, C, H, a.shape[-1]).transpose(1, 0, 3, 2, 4)

    qc, kc, vc, fc = to_chunks(q), to_chunks(k), to_chunks(v), to_chunks(logf)
    causal = jnp.tril(jnp.ones((C, C), dtype=bool))[:, :, None]

    def step(state, inp):
        qi, ki, vi, fi = inp
        b = jnp.cumsum(fi, axis=2)
        o_inter = jnp.einsum('bhtk,bhkv->bhtv', qi * jnp.exp(b), state)
        diff = b[:, :, :, None, :] - b[:, :, None, :, :]
        decay = jnp.exp(jnp.where(causal, diff, -jnp.inf))
        A = jnp.einsum('bhtk,bhsk,bhtsk->bhts', qi, ki, decay)
        o_intra = jnp.einsum('bhts,bhsv->bhtv', A, vi)
        b_last = b[:, :, -1:, :]
        new_state = jnp.exp(b_last[:, :, 0, :])[..., None] * state + jnp.einsum(
            'bhsk,bhsv->bhkv', ki * jnp.exp(b_last - b), vi)
        return new_state, o_inter + o_intra

    s0 = jnp.zeros((B, H, DK, DV), jnp.float32)
    _, o = lax.scan(step, s0, (qc, kc, vc, fc))
    return o.transpose(1, 0, 3, 2, 4).reshape(B, S, H, DV)


def hgrn2_bidir(hq, hf_fwd, hf_bwd, hi, hg, lb_fwd, lb_bwd, out_norm):
    B, S, _ = hq.shape
    heads = lambda a: a.reshape(B, S, HG_HEADS, -1)
    q = heads(jax.nn.silu(hq.astype(jnp.float32)))
    v = heads(hi.astype(jnp.float32))

    def gates(fpre, lb):
        f = lb + (1.0 - lb) * jax.nn.sigmoid(fpre.astype(jnp.float32))
        return heads(jnp.log(f)), heads(1.0 - f)

    logf_f, k_f = gates(hf_fwd, lb_fwd)
    logf_b, k_b = gates(hf_bwd, lb_bwd)
    o_f = gla_scan(q, k_f, v, logf_f)
    flip = lambda a: jnp.flip(a, axis=1)
    o_b = flip(gla_scan(flip(q), flip(k_b), flip(v), flip(logf_b)))
    o = o_f + o_b
    o = o * lax.rsqrt(jnp.mean(o * o, axis=-1, keepdims=True) + EPS) * out_norm.astype(jnp.float32)
    o = o.reshape(B, S, HG_WIDTH) * jax.nn.silu(hg.astype(jnp.float32))
    return o.astype(hq.dtype)


def rope(x, cos, sin):
    x1, x2 = jnp.split(x, 2, axis=-1)
    return jnp.concatenate([x1 * cos - x2 * sin, x2 * cos + x1 * sin], axis=-1)


def window_gqa(aq, ak, av, q_norm, k_norm, sink):
    B, S, _ = aq.shape
    dt = aq.dtype
    NB = S // BLOCK
    G = AT_HEADS // AT_KV_HEADS
    q = aq.reshape(B, S, AT_HEADS, AT_HEAD_DIM)
    k = ak.reshape(B, S, AT_KV_HEADS, AT_HEAD_DIM)
    v = av.reshape(B, S, AT_KV_HEADS, AT_HEAD_DIM)
    pos = jnp.arange(S, dtype=jnp.float32)
    inv_freq = ROPE_THETA ** (-jnp.arange(0, AT_HEAD_DIM, 2, dtype=jnp.float32) / AT_HEAD_DIM)
    ang = pos[:, None] * inv_freq[None, :]
    cos, sin = jnp.cos(ang)[:, None, :], jnp.sin(ang)[:, None, :]
    q = rope(rmsnorm(q, q_norm).astype(jnp.float32), cos, sin).astype(dt)
    k = rope(rmsnorm(k, k_norm).astype(jnp.float32), cos, sin).astype(dt)

    qb = q.reshape(B, NB, BLOCK, AT_KV_HEADS, G, AT_HEAD_DIM)
    padw = ((0, 0), (BLOCK, BLOCK), (0, 0), (0, 0))

    def band(a):
        ap = jnp.pad(a, padw).reshape(B, NB + 2, BLOCK, AT_KV_HEADS, AT_HEAD_DIM)
        return jnp.concatenate([ap[:, :-2], ap[:, 1:-1], ap[:, 2:]], axis=2)

    kb, vb = band(k), band(v)
    scale = 1.0 / math.sqrt(AT_HEAD_DIM)
    s = jnp.einsum('bnqhgd,bnkhd->bnhgqk', qb, kb, preferred_element_type=jnp.float32) * scale
    kpos = jnp.arange(3 * BLOCK) - BLOCK
    rel = kpos[None, :] - jnp.arange(BLOCK)[:, None]
    abs_k = jnp.arange(NB)[:, None, None] * BLOCK + kpos[None, None, :]
    mask = (jnp.abs(rel) <= WINDOW)[None] & (abs_k >= 0) & (abs_k < S)
    s = jnp.where(mask[None, :, None, None], s, -jnp.inf)
    sink_l = sink.astype(jnp.float32).reshape(AT_KV_HEADS, G)[None, None, :, :, None, None]
    m = jnp.maximum(jnp.max(s, axis=-1, keepdims=True), sink_l)
    p = jnp.exp(s - m)
    p = p / (jnp.sum(p, axis=-1, keepdims=True) + jnp.exp(sink_l - m))
    o = jnp.einsum('bnhgqk,bnkhd->bnqhgd', p.astype(dt), vb)
    return o.reshape(B, S, AT_WIDTH)


def setup_inputs(seed: int = 0) -> dict:
    key = jax.random.key(seed)
    ks = jax.random.split(key, 17)
    nrm = lambda k, shape, s: jax.random.normal(k, shape, jnp.float32) * s
    gain = lambda k, shape: 1.0 + nrm(k, shape, 0.05)
    return {
        "x_prompt": nrm(ks[0], (BATCH, SEQ, D_MODEL), 1.0),
        "x_sample": nrm(ks[1], (DEC_BATCH, DEC_SEQ, D_MODEL), 1.0),
        "ffn1_norm": gain(ks[2], (DEPTH, D_MODEL)),
        "ffn1_wi": nrm(ks[3], (DEPTH, D_MODEL, 2 * D_FF), D_MODEL ** -0.5),
        "ffn1_wo": nrm(ks[4], (DEPTH, D_FF, D_MODEL), D_FF ** -0.5),
        "mix_norm": gain(ks[5], (DEPTH, D_MODEL)),
        "w_in": nrm(ks[6], (DEPTH, D_MODEL, N_IN), D_MODEL ** -0.5),
        "hg_lb_fwd": nrm(ks[7], (DEPTH + 1, HG_WIDTH), 0.5),
        "hg_lb_bwd": nrm(ks[8], (DEPTH + 1, HG_WIDTH), 0.5),
        "hg_out_norm": gain(ks[9], (DEPTH, HG_DV)),
        "q_norm": gain(ks[10], (DEPTH, AT_HEAD_DIM)),
        "k_norm": gain(ks[11], (DEPTH, AT_HEAD_DIM)),
        "attn_sink": nrm(ks[12], (DEPTH, AT_HEADS), 0.5),
        "w_out": nrm(ks[13], (DEPTH, MIX_WIDTH, D_MODEL), MIX_WIDTH ** -0.5),
        "ffn2_norm": gain(ks[14], (DEPTH, D_MODEL)),
        "ffn2_wi": nrm(ks[15], (DEPTH, D_MODEL, 2 * D_FF), D_MODEL ** -0.5),
        "ffn2_wo": nrm(ks[16], (DEPTH, D_FF, D_MODEL), D_FF ** -0.5),
    }


def reference(x_prompt, x_sample, ffn1_norm, ffn1_wi, ffn1_wo, mix_norm, w_in, hg_lb_fwd, hg_lb_bwd,
              hg_out_norm, q_norm, k_norm, attn_sink, w_out, ffn2_norm, ffn2_wi, ffn2_wo):
    lb_f_all = jnp.cumsum(jax.nn.softmax(hg_lb_fwd.astype(jnp.float32), axis=0), axis=0)
    lb_b_all = jnp.cumsum(jax.nn.softmax(hg_lb_bwd.astype(jnp.float32), axis=0), axis=0)
    splits = np.cumsum([HG_WIDTH] * 5 + [AT_WIDTH, AT_KV_HEADS * AT_HEAD_DIM])

    def trunk(x):
        for l in range(DEPTH):
            x = x + 0.5 * swiglu(rmsnorm(x, ffn1_norm[l]), ffn1_wi[l], ffn1_wo[l])
            h = rmsnorm(x, mix_norm[l])
            proj = h @ w_in[l].astype(h.dtype)
            hq, hf_f, hf_b, hi, hg, aq, ak, av = jnp.split(proj, splits, axis=-1)
            o_hg = hgrn2_bidir(hq, hf_f, hf_b, hi, hg, lb_f_all[l], lb_b_all[l], hg_out_norm[l])
            o_at = window_gqa(aq, ak, av, q_norm[l], k_norm[l], attn_sink[l])
            mixed = jnp.concatenate([o_hg, o_at], axis=-1)
            x = x + mixed @ w_out[l].astype(x.dtype)
            x = x + 0.5 * swiglu(rmsnorm(x, ffn2_norm[l]), ffn2_wi[l], ffn2_wo[l])
        return x

    y_prompt = trunk(x_prompt)
    y_sample = trunk(x_sample)
    return (y_prompt, y_sample)
```

```python
import functools
import math

import jax
import jax.numpy as jnp
import numpy as np
from jax import lax
from jax.experimental import pallas as pl
from jax.experimental.pallas import tpu as pltpu

D_MODEL = 1024
D_FF = 2816
HG_WIDTH = 512
HG_HEADS = 4
HG_DK = 128
AT_WIDTH = 512
AT_HEAD_DIM = 64
AT_HEADS = 8
AT_KV_HEADS = 2
AT_GROUP = AT_HEADS // AT_KV_HEADS
AT_KV_WIDTH = AT_KV_HEADS * AT_HEAD_DIM
WINDOW = 128
ROPE_THETA = 10000.0
EPS = 1e-6

COL_HQ = 0
COL_HF_FWD = COL_HQ + HG_WIDTH
COL_HF_BWD = COL_HF_FWD + HG_WIDTH
COL_HI = COL_HF_BWD + HG_WIDTH
COL_HG = COL_HI + HG_WIDTH
COL_AQ = COL_HG + HG_WIDTH
COL_AKV = COL_AQ + AT_WIDTH
N_IN = COL_AKV + 2 * AT_KV_WIDTH

V7X_LANES = 128
V7X_MXU_COLS = 256
V7X_VMEM_LIMIT = 58 * 1024 * 1024

TOKEN_TILE = 512
FF_CHUNK = V7X_MXU_COLS
SCAN_CHUNK = 128
SCAN_LEVELS = int(math.log2(SCAN_CHUNK))
ATT_BLOCK = WINDOW

BF16 = jnp.bfloat16
F32 = jnp.float32


def _dot(a, b):
    return jnp.dot(a, b, preferred_element_type=F32)


def _dot_nt(a, b):
    return lax.dot_general(a, b, (((1,), (1,)), ((), ())), preferred_element_type=F32)


def _dot_tn(a, b):
    return lax.dot_general(a, b, (((0,), (0,)), ((), ())), preferred_element_type=F32)


def _rmsnorm(x, gain):
    ms = jnp.mean(x * x, axis=-1, keepdims=True)
    return x * lax.rsqrt(ms + EPS) * gain


def _silu(x):
    return x * jax.nn.sigmoid(x)


def _swiglu_ffn(h, wi_ref, wo_ref):
    acc = None
    for c in range(D_FF // FF_CHUNK):
        lo = c * FF_CHUNK
        gate = _dot(h, wi_ref[:, lo:lo + FF_CHUNK])
        up = _dot(h, wi_ref[:, D_FF + lo:D_FF + lo + FF_CHUNK])
        act = (_silu(gate) * up).astype(BF16)
        part = _dot(act, wo_ref[lo:lo + FF_CHUNK, :])
        acc = part if acc is None else acc + part
    return acc


def _log_forget(fpre, lb_ref):
    a0 = lb_ref[0:1, :]
    a1 = lb_ref[1:2, :]
    m = jnp.maximum(a0, a1)
    e0 = jnp.exp(a0 - m)
    e1 = jnp.exp(a1 - m)
    lb = e0 / (e0 + e1)
    return jnp.log(lb + (1.0 - lb) * jax.nn.sigmoid(fpre))


def _head_mean_square(x, seg_ref):
    sq = x * x
    hi = sq.astype(BF16)
    lo = (sq - hi.astype(F32)).astype(BF16)
    n = x.shape[1]
    seg = seg_ref[0:n, 0:n]
    return _dot(hi, seg) + _dot(lo, seg)


def _rope(x, cos, sin_signed):
    lane = lax.broadcasted_iota(jnp.int32, cos.shape, 1)
    first_half = (lane & (AT_HEAD_DIM // 2)) == 0
    outs = []
    for j in range(x.shape[1] // V7X_LANES):
        xs = x[:, j * V7X_LANES:(j + 1) * V7X_LANES]
        partner = jnp.where(first_half,
                            pltpu.roll(xs, V7X_LANES - AT_HEAD_DIM // 2, axis=1),
                            pltpu.roll(xs, AT_HEAD_DIM // 2, axis=1))
        outs.append(xs * cos + partner * sin_signed)
    return outs[0] if len(outs) == 1 else jnp.concatenate(outs, axis=1)


def _qk_norm_rope(x, gain, cos, sin_signed, seg_ref):
    xn = x * lax.rsqrt(_head_mean_square(x, seg_ref) + EPS) * gain
    return _rope(xn, cos, sin_signed).astype(BF16)


def _ffn_in_kernel(x_ref, n1_ref, wi_ref, wo_ref, n2_ref, win_ref, lbf_ref, lbb_ref, qn_ref, kn_ref,
                   cos_ref, sin_ref, seg_ref,
                   x1_ref, q_ref, lff_ref, lfb_ref, v_ref, g_ref, aq_ref, ak_ref, av_ref):
    x = x_ref[...]
    h = _rmsnorm(x, n1_ref[...]).astype(BF16)
    x1 = x + 0.5 * _swiglu_ffn(h, wi_ref, wo_ref)
    x1_ref[...] = x1
    h2 = _rmsnorm(x1, n2_ref[...]).astype(BF16)

    def proj(lo, width):
        return _dot(h2, win_ref[:, lo:lo + width])

    q_ref[...] = _silu(proj(COL_HQ, HG_WIDTH))
    lff_ref[...] = _log_forget(proj(COL_HF_FWD, HG_WIDTH), lbf_ref)
    lfb_ref[...] = _log_forget(proj(COL_HF_BWD, HG_WIDTH), lbb_ref)
    v_ref[...] = proj(COL_HI, HG_WIDTH).astype(BF16)
    g_ref[...] = _silu(proj(COL_HG, HG_WIDTH))
    cos = cos_ref[...]
    sin = sin_ref[...]
    aq_ref[...] = _qk_norm_rope(proj(COL_AQ, AT_WIDTH), qn_ref[...], cos, sin, seg_ref)
    akv = proj(COL_AKV, 2 * AT_KV_WIDTH)
    ak_ref[...] = _qk_norm_rope(akv[:, :AT_KV_WIDTH], kn_ref[...], cos, sin, seg_ref)
    av_ref[...] = akv[:, AT_KV_WIDTH:].astype(BF16)


def _cumsum_rows(x, reverse):
    n = x.shape[0]
    row = lax.broadcasted_iota(jnp.int32, x.shape, 0)
    s = 1
    while s < n:
        if reverse:
            x = x + jnp.where(row < n - s, pltpu.roll(x, n - s, axis=0), 0.0)
        else:
            x = x + jnp.where(row >= s, pltpu.roll(x, s, axis=0), 0.0)
        s *= 2
    return x


def _level_reference(x, half, reverse):
    n = x.shape[0]
    size = 2 * half
    ref_off = half if reverse else half - 1
    if size >= 8:
        pieces = [jnp.broadcast_to(x[b + ref_off:b + ref_off + 1, :], (size, x.shape[1]))
                  for b in range(0, n, size)]
        return pieces[0] if len(pieces) == 1 else jnp.concatenate(pieces, axis=0)
    row = lax.broadcasted_iota(jnp.int32, x.shape, 0)
    pos = row & (size - 1)
    out = x
    for j in range(size):
        d = ref_off - j
        if d != 0:
            out = jnp.where(pos == j, pltpu.roll(x, (n - d) % n, axis=0), out)
    return out


def _gla_chunk(q, lf, v, state_ref, lvl, reverse):
    n = q.shape[0]
    k = 1.0 - jnp.exp(lf)
    x = _cumsum_rows(lf, reverse)
    x_end = x[0:1, :] if reverse else x[n - 1:n, :]
    state = state_ref[...]
    o = _dot_nt((q * jnp.exp(x)).astype(BF16), state.astype(BF16))
    k_end = (k * jnp.exp(x_end - x)).astype(BF16)
    state_ref[...] = state * jnp.exp(x_end) + _dot_tn(v, k_end)

    a = jnp.zeros((n, n), F32)
    for level in range(SCAN_LEVELS):
        e = jnp.exp(-jnp.abs(x - _level_reference(x, 1 << level, reverse)))
        p = _dot_nt((q * e).astype(BF16), (k * e).astype(BF16))
        a = jnp.where(lvl == level, p, a)
    o = o + _dot(a.astype(BF16), v)
    diag = jnp.sum(q * k, axis=-1, keepdims=True)
    return o + diag * v.astype(F32)


def _scan_kernel(lvl_f_ref, lvl_b_ref, qf_ref, lff_ref, vf_ref, qb_ref, lfb_ref, vb_ref,
                 of_ref, ob_ref, state_ref):
    @pl.when(pl.program_id(1) == 0)
    def _():
        state_ref[...] = jnp.zeros(state_ref.shape, F32)

    directions = ((qf_ref, lff_ref, vf_ref, of_ref, lvl_f_ref, False),
                  (qb_ref, lfb_ref, vb_ref, ob_ref, lvl_b_ref, True))
    for d, (q_ref, lf_ref, v_ref, o_ref, lvl_ref, reverse) in enumerate(directions):
        lvl = lvl_ref[...]
        for h in range(HG_HEADS):
            sl = slice(h * HG_DK, (h + 1) * HG_DK)
            o_ref[:, sl] = _gla_chunk(q_ref[:, sl], lf_ref[:, sl], v_ref[:, sl],
                                      state_ref.at[d * HG_HEADS + h], lvl, reverse)


def _attn_kernel(sink_ref, q_ref, kp_ref, kc_ref, kn_ref, vp_ref, vc_ref, vn_ref, o_ref):
    i = pl.program_id(1)
    nb = pl.num_programs(1)
    blk = ATT_BLOCK
    qpos = lax.broadcasted_iota(jnp.int32, (AT_GROUP * blk, 3 * blk), 0) & (blk - 1)
    kpos = lax.broadcasted_iota(jnp.int32, (AT_GROUP * blk, 3 * blk), 1) - blk
    rel = kpos - qpos
    abs_k = kpos + i * blk
    mask = (jnp.abs(rel) <= WINDOW) & (abs_k >= 0) & (abs_k < nb * blk)
    scale = 1.0 / math.sqrt(AT_HEAD_DIM)
    for kvh in range(AT_KV_HEADS):
        ksl = slice(kvh * AT_HEAD_DIM, (kvh + 1) * AT_HEAD_DIM)
        keys = jnp.concatenate([kp_ref[:, ksl], kc_ref[:, ksl], kn_ref[:, ksl]], axis=0)
        vals = jnp.concatenate([vp_ref[:, ksl], vc_ref[:, ksl], vn_ref[:, ksl]], axis=0)
        heads = [kvh * AT_GROUP + g for g in range(AT_GROUP)]
        q = jnp.concatenate([q_ref[:, hd * AT_HEAD_DIM:(hd + 1) * AT_HEAD_DIM] for hd in heads], axis=0)
        sink = jnp.concatenate([jnp.full((blk, 1), sink_ref[hd], F32) for hd in heads], axis=0)
        s = jnp.where(mask, _dot_nt(q, keys) * scale, -jnp.inf)
        m = jnp.maximum(jnp.max(s, axis=-1, keepdims=True), sink)
        p = jnp.exp(s - m)
        denom = jnp.sum(p, axis=-1, keepdims=True) + jnp.exp(sink - m)
        o = _dot(p.astype(BF16), vals) / denom
        for g, hd in enumerate(heads):
            o_ref[:, hd * AT_HEAD_DIM:(hd + 1) * AT_HEAD_DIM] = o[g * blk:(g + 1) * blk, :].astype(BF16)


def _ffn_out_kernel(x1_ref, of_ref, ob_ref, g_ref, oat_ref, onorm_ref, wout_ref, n3_ref, wi_ref, wo_ref,
                    y_ref):
    o = of_ref[...] + ob_ref[...]
    g = g_ref[...]
    onorm = onorm_ref[...]
    mixed = []
    for h in range(HG_HEADS):
        sl = slice(h * HG_DK, (h + 1) * HG_DK)
        mixed.append((_rmsnorm(o[:, sl], onorm) * g[:, sl]).astype(BF16))
    mixed.append(oat_ref[...])
    x2 = x1_ref[...] + _dot(jnp.concatenate(mixed, axis=1), wout_ref[...])
    h2 = _rmsnorm(x2, n3_ref[...]).astype(BF16)
    y_ref[...] = x2 + 0.5 * _swiglu_ffn(h2, wi_ref, wo_ref)


def _resident(shape):
    return pl.BlockSpec(shape, lambda *_: (0,) * len(shape), pipeline_mode=pl.Buffered(1))


def _params(semantics):
    return pltpu.CompilerParams(dimension_semantics=semantics, vmem_limit_bytes=V7X_VMEM_LIMIT)


def _scan_level_table(reverse):
    t = np.arange(SCAN_CHUNK)[:, None]
    s = np.arange(SCAN_CHUNK)[None, :]
    x = t ^ s
    lvl = np.where(x > 0, np.floor(np.log2(np.maximum(x, 1))).astype(np.int32), -1)
    valid = (s > t) if reverse else (s < t)
    return jnp.asarray(np.where(valid, lvl, -1).astype(np.int32))


def _rope_tables(seq):
    pos = jnp.arange(seq, dtype=F32)
    inv_freq = ROPE_THETA ** (-jnp.arange(0, AT_HEAD_DIM, 2, dtype=F32) / AT_HEAD_DIM)
    ang = pos[:, None] * inv_freq[None, :]
    cos, sin = jnp.cos(ang), jnp.sin(ang)
    reps = V7X_LANES // AT_HEAD_DIM
    return jnp.tile(cos, (1, 2 * reps)), jnp.tile(jnp.concatenate([-sin, sin], axis=1), (1, reps))


def _layer(x, w):
    batch, seq, _ = x.shape
    tokens = batch * seq
    tm = TOKEN_TILE
    x2d = x.reshape(tokens, D_MODEL)
    tiles_per_seq = seq // tm

    row = lambda width: pl.BlockSpec((tm, width), lambda i: (i, 0))
    pos = lambda width: pl.BlockSpec((tm, width), lambda i: (i % tiles_per_seq, 0))
    sds = lambda width, dt: jax.ShapeDtypeStruct((tokens, width), dt)

    x1, q, lff, lfb, v, g, aq, ak, av = pl.pallas_call(
        _ffn_in_kernel,
        name="ffn_in",
        grid=(tokens // tm,),
        in_specs=[row(D_MODEL), _resident((1, D_MODEL)), _resident((D_MODEL, 2 * D_FF)),
                  _resident((D_FF, D_MODEL)), _resident((1, D_MODEL)), _resident((D_MODEL, N_IN)),
                  _resident((2, HG_WIDTH)), _resident((2, HG_WIDTH)), _resident((1, AT_WIDTH)),
                  _resident((1, AT_KV_WIDTH)), pos(V7X_LANES), pos(V7X_LANES),
                  _resident((AT_WIDTH, AT_WIDTH))],
        out_specs=[row(D_MODEL), row(HG_WIDTH), row(HG_WIDTH), row(HG_WIDTH), row(HG_WIDTH), row(HG_WIDTH),
                   row(AT_WIDTH), row(AT_KV_WIDTH), row(AT_KV_WIDTH)],
        out_shape=[sds(D_MODEL, F32), sds(HG_WIDTH, F32), sds(HG_WIDTH, F32), sds(HG_WIDTH, F32),
                   sds(HG_WIDTH, BF16), sds(HG_WIDTH, F32), sds(AT_WIDTH, BF16), sds(AT_KV_WIDTH, BF16),
                   sds(AT_KV_WIDTH, BF16)],
        compiler_params=_params(("parallel",)),
    )(x2d, w["ffn1_norm"], w["ffn1_wi"], w["ffn1_wo"], w["mix_norm"], w["w_in"], w["hg_lb_fwd"],
      w["hg_lb_bwd"], w["q_norm"], w["k_norm"], w["cos"], w["sin"], w["seg"])

    c = SCAN_CHUNK
    nc = seq // c
    fwd = lambda b, i: (b * nc + i, 0)
    bwd = lambda b, i: (b * nc + nc - 1 - i, 0)
    chunk = lambda imap: pl.BlockSpec((c, HG_WIDTH), imap)
    o_f, o_b = pl.pallas_call(
        _scan_kernel,
        name="hgrn_scan",
        grid=(batch, nc),
        in_specs=[_resident((c, c)), _resident((c, c)),
                  chunk(fwd), chunk(fwd), chunk(fwd), chunk(bwd), chunk(bwd), chunk(bwd)],
        out_specs=[chunk(fwd), chunk(bwd)],
        out_shape=[sds(HG_WIDTH, F32), sds(HG_WIDTH, F32)],
        scratch_shapes=[pltpu.VMEM((2 * HG_HEADS, HG_DK, HG_DK), F32)],
        compiler_params=_params(("parallel", "arbitrary")),
    )(w["lvl_fwd"], w["lvl_bwd"], q, lff, v, q, lfb, v)

    blk = ATT_BLOCK
    nb = seq // blk
    cur = lambda b, i: (b * nb + i, 0)
    prv = lambda b, i: (b * nb + jnp.maximum(i - 1, 0), 0)
    nxt = lambda b, i: (b * nb + jnp.minimum(i + 1, nb - 1), 0)
    kv = lambda imap: pl.BlockSpec((blk, AT_KV_WIDTH), imap)
    o_at = pl.pallas_call(
        _attn_kernel,
        name="win_attn",
        grid=(batch, nb),
        in_specs=[pl.BlockSpec(memory_space=pltpu.SMEM), pl.BlockSpec((blk, AT_WIDTH), cur),
                  kv(prv), kv(cur), kv(nxt), kv(prv), kv(cur), kv(nxt)],
        out_specs=pl.BlockSpec((blk, AT_WIDTH), cur),
        out_shape=sds(AT_WIDTH, BF16),
        compiler_params=_params(("parallel", "parallel")),
    )(w["attn_sink"], aq, ak, ak, ak, av, av, av)

    y = pl.pallas_call(
        _ffn_out_kernel,
        name="ffn_out",
        grid=(tokens // tm,),
        in_specs=[row(D_MODEL), row(HG_WIDTH), row(HG_WIDTH), row(HG_WIDTH), row(AT_WIDTH),
                  _resident((1, HG_DK)), _resident((D_MODEL, D_MODEL)), _resident((1, D_MODEL)),
                  _resident((D_MODEL, 2 * D_FF)), _resident((D_FF, D_MODEL))],
        out_specs=row(D_MODEL),
        out_shape=sds(D_MODEL, F32),
        compiler_params=_params(("parallel",)),
    )(x1, o_f, o_b, g, o_at, w["hg_out_norm"], w["w_out"], w["ffn2_norm"], w["ffn2_wi"], w["ffn2_wo"])
    return y.reshape(batch, seq, D_MODEL)


def kernel(x_prompt, x_sample, ffn1_norm, ffn1_wi, ffn1_wo, mix_norm, w_in, hg_lb_fwd, hg_lb_bwd, hg_out_norm,
           q_norm, k_norm, attn_sink, w_out, ffn2_norm, ffn2_wi, ffn2_wo):
    seq = x_prompt.shape[1]
    assert x_sample.shape[1] == seq and seq % TOKEN_TILE == 0
    cos, sin = _rope_tables(seq)
    seg = jnp.kron(jnp.eye(AT_HEADS, dtype=F32), jnp.full((AT_HEAD_DIM, AT_HEAD_DIM), 1.0 / AT_HEAD_DIM, F32))
    w = {
        "ffn1_norm": ffn1_norm[0][None, :], "ffn1_wi": ffn1_wi[0].astype(BF16), "ffn1_wo": ffn1_wo[0].astype(BF16),
        "mix_norm": mix_norm[0][None, :], "w_in": w_in[0].astype(BF16),
        "hg_lb_fwd": hg_lb_fwd, "hg_lb_bwd": hg_lb_bwd, "hg_out_norm": hg_out_norm[0][None, :],
        "q_norm": jnp.tile(q_norm[0], AT_HEADS)[None, :], "k_norm": jnp.tile(k_norm[0], AT_KV_HEADS)[None, :],
        "attn_sink": attn_sink[0], "w_out": w_out[0].astype(BF16),
        "ffn2_norm": ffn2_norm[0][None, :], "ffn2_wi": ffn2_wi[0].astype(BF16), "ffn2_wo": ffn2_wo[0].astype(BF16),
        "cos": cos, "sin": sin, "seg": seg.astype(BF16),
        "lvl_fwd": _scan_level_table(False), "lvl_bwd": _scan_level_table(True),
    }
    return (_layer(x_prompt, w), _layer(x_sample, w))
```

```python
import functools
import math

import jax
import jax.numpy as jnp
import numpy as np
from jax import lax
from jax.experimental import pallas as pl
from jax.experimental.pallas import tpu as pltpu

D_MODEL = 1024
D_FF = 2816
HG_WIDTH = 512
HG_HEADS = 4
HG_DK = 128
AT_WIDTH = 512
AT_HEAD_DIM = 64
AT_HEADS = 8
AT_KV_HEADS = 2
AT_GROUP = AT_HEADS // AT_KV_HEADS
AT_KV_WIDTH = AT_KV_HEADS * AT_HEAD_DIM
WINDOW = 128
ROPE_THETA = 10000.0
EPS = 1e-6

COL_HQ = 0
COL_HF_FWD = COL_HQ + HG_WIDTH
COL_HF_BWD = COL_HF_FWD + HG_WIDTH
COL_HI = COL_HF_BWD + HG_WIDTH
COL_HG = COL_HI + HG_WIDTH
COL_AQ = COL_HG + HG_WIDTH
COL_AKV = COL_AQ + AT_WIDTH
N_IN = COL_AKV + 2 * AT_KV_WIDTH

V7X_LANES = 128
V7X_MXU_COLS = 256
V7X_VMEM_LIMIT = 58 * 1024 * 1024

TOKEN_TILE = 512
FF_CHUNK = V7X_MXU_COLS
SCAN_CHUNK = 128
SCAN_LEVELS = int(math.log2(SCAN_CHUNK))
ATT_BLOCK = WINDOW

BF16 = jnp.bfloat16
F32 = jnp.float32


def _dot(a, b):
    return jnp.dot(a, b, preferred_element_type=F32)


def _dot_nt(a, b):
    return lax.dot_general(a, b, (((1,), (1,)), ((), ())), preferred_element_type=F32)


def _dot_tn(a, b):
    return lax.dot_general(a, b, (((0,), (0,)), ((), ())), preferred_element_type=F32)


def _rmsnorm(x, gain):
    ms = jnp.mean(x * x, axis=-1, keepdims=True)
    return x * lax.rsqrt(ms + EPS) * gain


def _silu(x):
    return x * jax.nn.sigmoid(x)


def _swiglu_ffn(h, wi_ref, wo_ref):
    acc = None
    for c in range(D_FF // FF_CHUNK):
        lo = c * FF_CHUNK
        gate = _dot(h, wi_ref[:, lo:lo + FF_CHUNK])
        up = _dot(h, wi_ref[:, D_FF + lo:D_FF + lo + FF_CHUNK])
        act = (_silu(gate) * up).astype(BF16)
        part = _dot(act, wo_ref[lo:lo + FF_CHUNK, :])
        acc = part if acc is None else acc + part
    return acc


def _log2_forget(fpre, lb_ref):
    a0 = lb_ref[0:1, :]
    a1 = lb_ref[1:2, :]
    m = jnp.maximum(a0, a1)
    e0 = jnp.exp(a0 - m)
    e1 = jnp.exp(a1 - m)
    lb = e0 / (e0 + e1)
    return jnp.log(lb + (1.0 - lb) * jax.nn.sigmoid(fpre)) * (1.0 / math.log(2.0))


def _head_mean_square(x, seg_ref):
    sq = x * x
    hi = sq.astype(BF16)
    lo = (sq - hi.astype(F32)).astype(BF16)
    n = x.shape[1]
    seg = seg_ref[0:n, 0:n]
    return _dot(hi, seg) + _dot(lo, seg)


def _rope(x, cos, sin_signed):
    lane = lax.broadcasted_iota(jnp.int32, cos.shape, 1)
    first_half = (lane & (AT_HEAD_DIM // 2)) == 0
    outs = []
    for j in range(x.shape[1] // V7X_LANES):
        xs = x[:, j * V7X_LANES:(j + 1) * V7X_LANES]
        partner = jnp.where(first_half,
                            pltpu.roll(xs, V7X_LANES - AT_HEAD_DIM // 2, axis=1),
                            pltpu.roll(xs, AT_HEAD_DIM // 2, axis=1))
        outs.append(xs * cos + partner * sin_signed)
    return outs[0] if len(outs) == 1 else jnp.concatenate(outs, axis=1)


def _qk_norm_rope(x, gain, cos, sin_signed, seg_ref):
    xn = x * lax.rsqrt(_head_mean_square(x, seg_ref) + EPS) * gain
    return _rope(xn, cos, sin_signed)


def _ffn_in_kernel(x_ref, n1_ref, wi_ref, wo_ref, n2_ref, win_ref, lbf_ref, lbb_ref, qn_ref, kn_ref,
                   cos_ref, sin_ref, seg_ref,
                   x1_ref, q_ref, lff_ref, lfb_ref, v_ref, g_ref, aq_ref, akv_ref):
    x = x_ref[...]
    h = _rmsnorm(x, n1_ref[...]).astype(BF16)
    x1 = x + 0.5 * _swiglu_ffn(h, wi_ref, wo_ref)
    x1_ref[...] = x1
    h2 = _rmsnorm(x1, n2_ref[...]).astype(BF16)

    def proj(lo, width):
        return _dot(h2, win_ref[:, lo:lo + width])

    q_ref[...] = _silu(proj(COL_HQ, HG_WIDTH))
    lff_ref[...] = _log2_forget(proj(COL_HF_FWD, HG_WIDTH), lbf_ref)
    lfb_ref[...] = _log2_forget(proj(COL_HF_BWD, HG_WIDTH), lbb_ref)
    v_ref[...] = proj(COL_HI, HG_WIDTH).astype(BF16)
    g_ref[...] = _silu(proj(COL_HG, HG_WIDTH))
    cos = cos_ref[...]
    sin = sin_ref[...]
    aq = _qk_norm_rope(proj(COL_AQ, AT_WIDTH), qn_ref[...], cos, sin, seg_ref) * (1.0 / math.sqrt(AT_HEAD_DIM))
    aq_ref[...] = aq.astype(BF16)
    akv = proj(COL_AKV, 2 * AT_KV_WIDTH)
    ak = _qk_norm_rope(akv[:, :AT_KV_WIDTH], kn_ref[...], cos, sin, seg_ref)
    av = akv[:, AT_KV_WIDTH:]
    akv_ref[...] = jnp.concatenate(
        [ak, pltpu.roll(ak, AT_HEAD_DIM, axis=1), av, pltpu.roll(av, AT_HEAD_DIM, axis=1)], axis=1).astype(BF16)


SUBLANES = 8
GROUP_LEVELS = 3


def _cumsum_rows(lf, tri_ref):
    hi = lf.astype(BF16)
    lo = (lf - hi.astype(F32)).astype(BF16)
    tri = tri_ref[...]
    return _dot(tri, hi) + _dot(tri, lo)


def _groups(a):
    return [a[g * SUBLANES:(g + 1) * SUBLANES, :] for g in range(a.shape[0] // SUBLANES)]


def _is_target_group(g, level, reverse):
    bit = (g >> (level - GROUP_LEVELS)) & 1
    return bit == (0 if reverse else 1)


def _gla_intra(q, lf, x, lvl, reverse):
    n = q.shape[0]
    ngroups = n // SUBLANES
    f = jnp.exp2(lf)
    k = 1.0 - f
    row = lax.broadcasted_iota(jnp.int32, q.shape, 0)
    lvl_g = _groups(lvl)
    a_g = [None] * ngroups

    def place(g, level, p_rows):
        keep = jnp.zeros_like(p_rows) if a_g[g] is None else a_g[g]
        a_g[g] = jnp.where(lvl_g[g] == level, p_rows, keep)

    for level in range(GROUP_LEVELS, SCAN_LEVELS):
        half = 1 << level
        q_rows, k_rows, targets = [], [], []
        for b in range(0, n, 2 * half):
            first, second = slice(b, b + half), slice(b + half, b + 2 * half)
            src, tgt = (second, first) if reverse else (first, second)
            r = b + half if reverse else b + half - 1
            x_r = x[r:r + 1, :]
            q_rows.append(q[tgt] * jnp.exp2(x[tgt] - x_r))
            k_src = k[src] * jnp.exp2(x_r - x[src])
            k_rows += [jnp.zeros_like(k_src), k_src] if reverse else [k_src, jnp.zeros_like(k_src)]
            targets += [g for g in range(b // SUBLANES, (b + 2 * half) // SUBLANES)
                        if _is_target_group(g, level, reverse)]
        p = _dot_nt(jnp.concatenate(q_rows, axis=0).astype(BF16), jnp.concatenate(k_rows, axis=0).astype(BF16))
        for i, g in enumerate(targets):
            place(g, level, p[i * SUBLANES:(i + 1) * SUBLANES, :])

    up = pltpu.roll(lf, n - 1, axis=0)
    down = pltpu.roll(lf, 1, axis=0)
    pos4 = row & 3
    if reverse:
        e1 = jnp.exp2(jnp.where(pos4 == 0, lf + up, jnp.where(pos4 == 1, lf, jnp.where(pos4 == 2, 0.0, down))))
        e0 = jnp.where((row & 1) == 0, f, 1.0)
        r2 = SUBLANES // 2
    else:
        e1 = jnp.exp2(jnp.where(pos4 == 0, up, jnp.where(pos4 == 1, 0.0, jnp.where(pos4 == 2, lf, lf + down))))
        e0 = jnp.where((row & 1) == 1, f, 1.0)
        r2 = SUBLANES // 2 - 1
    x_r2 = jnp.concatenate([jnp.broadcast_to(xg[r2:r2 + 1, :], xg.shape) for xg in _groups(x)], axis=0)
    e2 = jnp.exp2(-jnp.abs(x - x_r2))
    for level, e in enumerate((e0, e1, e2)):
        p = _dot_nt((q * e).astype(BF16), (k * e).astype(BF16))
        for g in range(ngroups):
            place(g, level, p[g * SUBLANES:(g + 1) * SUBLANES, :])

    a = jnp.concatenate(a_g, axis=0).astype(BF16)
    diag = jnp.sum(q * k, axis=-1, keepdims=True)
    x_end = x[0:1, :] if reverse else x[n - 1:n, :]
    q_in = (q * jnp.exp2(x)).astype(BF16)
    k_end = (k * jnp.exp2(x_end - x)).astype(BF16)
    return a, diag, q_in, k_end, jnp.exp2(x_end)


def _gla_inter(intra, v, state_ref):
    a, diag, q_in, k_end, decay = intra
    state = state_ref[...]
    o = _dot_nt(q_in, state.astype(BF16)) + _dot(a, v) + diag * v.astype(F32)
    state_ref[...] = state * decay + _dot_tn(v, k_end)
    return o


def _scan_kernel(lvl_f_ref, lvl_b_ref, tri_f_ref, tri_b_ref, qf_ref, lff_ref, vf_ref, qb_ref, lfb_ref, vb_ref,
                 of_ref, ob_ref, state_ref):
    @pl.when(pl.program_id(1) == 0)
    def _():
        state_ref[...] = jnp.zeros(state_ref.shape, F32)

    directions = ((qf_ref, lff_ref, vf_ref, of_ref, lvl_f_ref, tri_f_ref, False),
                  (qb_ref, lfb_ref, vb_ref, ob_ref, lvl_b_ref, tri_b_ref, True))
    heads = [(d, h) for d in range(2) for h in range(HG_HEADS)]
    cols = lambda h: slice(h * HG_DK, (h + 1) * HG_DK)
    cum = [_cumsum_rows(lf_ref[...], tri_ref) for _, lf_ref, _, _, _, tri_ref, _ in directions]
    intra = []
    for d, h in heads:
        q_ref, lf_ref, _, _, lvl_ref, _, reverse = directions[d]
        intra.append(_gla_intra(q_ref[:, cols(h)], lf_ref[:, cols(h)], cum[d][:, cols(h)], lvl_ref[...], reverse))
    for (d, h), part in zip(heads, intra):
        v_ref, o_ref = directions[d][2], directions[d][3]
        o_ref[:, cols(h)] = _gla_inter(part, v_ref[:, cols(h)], state_ref.at[d * HG_HEADS + h])


def _attn_kernel(sink_ref, band_ref, q_ref, kvp_ref, kvc_ref, kvn_ref, o_ref):
    i = pl.program_id(1)
    blk = ATT_BLOCK
    neg_inf = jnp.float32(-jnp.inf)
    bias_prev = band_ref[:, 0:blk] + jnp.where(i == 0, neg_inf, 0.0)
    bias_next = band_ref[:, blk:2 * blk] + jnp.where(i == pl.num_programs(1) - 1, neg_inf, 0.0)
    lane = lax.broadcasted_iota(jnp.int32, (blk, V7X_LANES), 1)
    low = lane < AT_HEAD_DIM

    def kv_cat(col):
        sl = slice(col * V7X_LANES, (col + 1) * V7X_LANES)
        return jnp.concatenate([kvp_ref[:, sl], kvc_ref[:, sl], kvn_ref[:, sl]], axis=0)

    keys = (kv_cat(0), kv_cat(1))
    vals = (kv_cat(2), kv_cat(3))

    def swapped(hd):
        return int(hd % 2 != hd // AT_GROUP)

    def scores(hd):
        q_pair = q_ref[:, (hd // 2) * V7X_LANES:(hd // 2 + 1) * V7X_LANES]
        q = jnp.where(low if hd % 2 == 0 else ~low, q_pair, jnp.zeros_like(q_pair))
        return _dot_nt(q, keys[swapped(hd)])

    all_scores = [scores(hd) for hd in range(AT_HEADS)]
    for pair in range(AT_HEADS // 2):
        res = []
        for side in range(2):
            hd = 2 * pair + side
            s = all_scores[hd]
            s_prev = s[:, 0:blk] + bias_prev
            s_cur = s[:, blk:2 * blk]
            s_next = s[:, 2 * blk:3 * blk] + bias_next
            sink = sink_ref[hd]
            m = jnp.max(jnp.maximum(jnp.maximum(s_prev, s_cur), s_next), axis=-1, keepdims=True)
            m = jnp.maximum(m, sink)
            p_prev = jnp.exp(s_prev - m)
            p_cur = jnp.exp(s_cur - m)
            p_next = jnp.exp(s_next - m)
            denom = jnp.sum(p_prev + p_cur + p_next, axis=-1, keepdims=True) + jnp.exp(sink - m)
            p = jnp.concatenate([p_prev, p_cur, p_next], axis=1).astype(BF16)
            res.append(_dot(p, vals[swapped(hd)]) / denom)
        o_ref[:, pair * V7X_LANES:(pair + 1) * V7X_LANES] = jnp.where(low, res[0], res[1]).astype(BF16)


def _ffn_out_kernel(x1_ref, of_ref, ob_ref, g_ref, oat_ref, onorm_ref, wout_ref, n3_ref, wi_ref, wo_ref,
                    y_ref):
    o = of_ref[...] + ob_ref[...]
    g = g_ref[...]
    onorm = onorm_ref[...]
    mixed = []
    for h in range(HG_HEADS):
        sl = slice(h * HG_DK, (h + 1) * HG_DK)
        mixed.append((_rmsnorm(o[:, sl], onorm) * g[:, sl]).astype(BF16))
    mixed.append(oat_ref[...])
    x2 = x1_ref[...] + _dot(jnp.concatenate(mixed, axis=1), wout_ref[...])
    h2 = _rmsnorm(x2, n3_ref[...]).astype(BF16)
    y_ref[...] = x2 + 0.5 * _swiglu_ffn(h2, wi_ref, wo_ref)


def _resident(shape):
    return pl.BlockSpec(shape, lambda *_: (0,) * len(shape), pipeline_mode=pl.Buffered(1))


def _params(semantics):
    return pltpu.CompilerParams(dimension_semantics=semantics, vmem_limit_bytes=V7X_VMEM_LIMIT)


def _scan_level_table(reverse):
    t = np.arange(SCAN_CHUNK)[:, None]
    s = np.arange(SCAN_CHUNK)[None, :]
    x = t ^ s
    lvl = np.where(x > 0, np.floor(np.log2(np.maximum(x, 1))).astype(np.int32), -1)
    valid = (s > t) if reverse else (s < t)
    return jnp.asarray(np.where(valid, lvl, -1).astype(np.int32))


def _band_bias():
    r = np.arange(ATT_BLOCK)[:, None]
    c = np.arange(ATT_BLOCK)[None, :]
    prev = np.where(c >= r, 0.0, -np.inf)
    nxt = np.where(c <= r, 0.0, -np.inf)
    return jnp.asarray(np.concatenate([prev, nxt], axis=1).astype(np.float32))


def _rope_tables(seq):
    pos = jnp.arange(seq, dtype=F32)
    inv_freq = ROPE_THETA ** (-jnp.arange(0, AT_HEAD_DIM, 2, dtype=F32) / AT_HEAD_DIM)
    ang = pos[:, None] * inv_freq[None, :]
    cos, sin = jnp.cos(ang), jnp.sin(ang)
    reps = V7X_LANES // AT_HEAD_DIM
    return jnp.tile(cos, (1, 2 * reps)), jnp.tile(jnp.concatenate([-sin, sin], axis=1), (1, reps))


def _layer(x, w):
    batch, seq, _ = x.shape
    tokens = batch * seq
    tm = TOKEN_TILE
    x2d = x.reshape(tokens, D_MODEL)
    tiles_per_seq = seq // tm

    row = lambda width: pl.BlockSpec((tm, width), lambda i: (i, 0))
    pos = lambda width: pl.BlockSpec((tm, width), lambda i: (i % tiles_per_seq, 0))
    sds = lambda width, dt: jax.ShapeDtypeStruct((tokens, width), dt)

    x1, q, lff, lfb, v, g, aq, akv = pl.pallas_call(
        _ffn_in_kernel,
        name="ffn_in",
        grid=(tokens // tm,),
        in_specs=[row(D_MODEL), _resident((1, D_MODEL)), _resident((D_MODEL, 2 * D_FF)),
                  _resident((D_FF, D_MODEL)), _resident((1, D_MODEL)), _resident((D_MODEL, N_IN)),
                  _resident((2, HG_WIDTH)), _resident((2, HG_WIDTH)), _resident((1, AT_WIDTH)),
                  _resident((1, AT_KV_WIDTH)), pos(V7X_LANES), pos(V7X_LANES),
                  _resident((AT_WIDTH, AT_WIDTH))],
        out_specs=[row(D_MODEL), row(HG_WIDTH), row(HG_WIDTH), row(HG_WIDTH), row(HG_WIDTH), row(HG_WIDTH),
                   row(AT_WIDTH), row(4 * AT_KV_WIDTH)],
        out_shape=[sds(D_MODEL, F32), sds(HG_WIDTH, F32), sds(HG_WIDTH, F32), sds(HG_WIDTH, F32),
                   sds(HG_WIDTH, BF16), sds(HG_WIDTH, F32), sds(AT_WIDTH, BF16), sds(4 * AT_KV_WIDTH, BF16)],
        compiler_params=_params(("parallel",)),
    )(x2d, w["ffn1_norm"], w["ffn1_wi"], w["ffn1_wo"], w["mix_norm"], w["w_in"], w["hg_lb_fwd"],
      w["hg_lb_bwd"], w["q_norm"], w["k_norm"], w["cos"], w["sin"], w["seg"])

    c = SCAN_CHUNK
    nc = seq // c
    fwd = lambda b, i: (b * nc + i, 0)
    bwd = lambda b, i: (b * nc + nc - 1 - i, 0)
    chunk = lambda imap: pl.BlockSpec((c, HG_WIDTH), imap)
    o_f, o_b = pl.pallas_call(
        _scan_kernel,
        name="hgrn_scan",
        grid=(batch, nc),
        in_specs=[_resident((c, c)), _resident((c, c)), _resident((c, c)), _resident((c, c)),
                  chunk(fwd), chunk(fwd), chunk(fwd), chunk(bwd), chunk(bwd), chunk(bwd)],
        out_specs=[chunk(fwd), chunk(bwd)],
        out_shape=[sds(HG_WIDTH, F32), sds(HG_WIDTH, F32)],
        scratch_shapes=[pltpu.VMEM((2 * HG_HEADS, HG_DK, HG_DK), F32)],
        compiler_params=_params(("parallel", "arbitrary")),
    )(w["lvl_fwd"], w["lvl_bwd"], w["tri_fwd"], w["tri_bwd"], q, lff, v, q, lfb, v)

    blk = ATT_BLOCK
    nb = seq // blk
    cur = lambda b, i: (b * nb + i, 0)
    prv = lambda b, i: (b * nb + jnp.maximum(i - 1, 0), 0)
    nxt = lambda b, i: (b * nb + jnp.minimum(i + 1, nb - 1), 0)
    kv = lambda imap: pl.BlockSpec((blk, 4 * AT_KV_WIDTH), imap)
    o_at = pl.pallas_call(
        _attn_kernel,
        name="win_attn",
        grid=(batch, nb),
        in_specs=[pl.BlockSpec(memory_space=pltpu.SMEM), _resident((blk, 2 * blk)),
                  pl.BlockSpec((blk, AT_WIDTH), cur), kv(prv), kv(cur), kv(nxt)],
        out_specs=pl.BlockSpec((blk, AT_WIDTH), cur),
        out_shape=sds(AT_WIDTH, BF16),
        compiler_params=_params(("parallel", "parallel")),
    )(w["attn_sink"], w["band"], aq, akv, akv, akv)

    y = pl.pallas_call(
        _ffn_out_kernel,
        name="ffn_out",
        grid=(tokens // tm,),
        in_specs=[row(D_MODEL), row(HG_WIDTH), row(HG_WIDTH), row(HG_WIDTH), row(AT_WIDTH),
                  _resident((1, HG_DK)), _resident((D_MODEL, D_MODEL)), _resident((1, D_MODEL)),
                  _resident((D_MODEL, 2 * D_FF)), _resident((D_FF, D_MODEL))],
        out_specs=row(D_MODEL),
        out_shape=sds(D_MODEL, F32),
        compiler_params=_params(("parallel",)),
    )(x1, o_f, o_b, g, o_at, w["hg_out_norm"], w["w_out"], w["ffn2_norm"], w["ffn2_wi"], w["ffn2_wo"])
    return y.reshape(batch, seq, D_MODEL)


def kernel(x_prompt, x_sample, ffn1_norm, ffn1_wi, ffn1_wo, mix_norm, w_in, hg_lb_fwd, hg_lb_bwd, hg_out_norm,
           q_norm, k_norm, attn_sink, w_out, ffn2_norm, ffn2_wi, ffn2_wo):
    seq = x_prompt.shape[1]
    assert x_sample.shape[1] == seq and seq % TOKEN_TILE == 0
    cos, sin = _rope_tables(seq)
    seg = jnp.kron(jnp.eye(AT_HEADS, dtype=F32), jnp.full((AT_HEAD_DIM, AT_HEAD_DIM), 1.0 / AT_HEAD_DIM, F32))
    w = {
        "ffn1_norm": ffn1_norm[0][None, :], "ffn1_wi": ffn1_wi[0].astype(BF16), "ffn1_wo": ffn1_wo[0].astype(BF16),
        "mix_norm": mix_norm[0][None, :], "w_in": w_in[0].astype(BF16),
        "hg_lb_fwd": hg_lb_fwd, "hg_lb_bwd": hg_lb_bwd, "hg_out_norm": hg_out_norm[0][None, :],
        "q_norm": jnp.tile(q_norm[0], AT_HEADS)[None, :], "k_norm": jnp.tile(k_norm[0], AT_KV_HEADS)[None, :],
        "attn_sink": attn_sink[0], "w_out": w_out[0].astype(BF16),
        "ffn2_norm": ffn2_norm[0][None, :], "ffn2_wi": ffn2_wi[0].astype(BF16), "ffn2_wo": ffn2_wo[0].astype(BF16),
        "cos": cos, "sin": sin, "seg": seg.astype(BF16),
        "lvl_fwd": _scan_level_table(False), "lvl_bwd": _scan_level_table(True), "band": _band_bias(),
        "tri_fwd": jnp.asarray(np.tril(np.ones((SCAN_CHUNK, SCAN_CHUNK), np.float32)), BF16),
        "tri_bwd": jnp.asarray(np.triu(np.ones((SCAN_CHUNK, SCAN_CHUNK), np.float32)), BF16),
    }
    return (_layer(x_prompt, w), _layer(x_sample, w))
```

```python
import functools
import math

import jax
import jax.numpy as jnp
import numpy as np
from jax import lax
from jax.experimental import pallas as pl
from jax.experimental.pallas import tpu as pltpu

D_MODEL = 1024
D_FF = 2816
HG_WIDTH = 512
HG_HEADS = 4
HG_DK = 128
AT_WIDTH = 512
AT_HEAD_DIM = 64
AT_HEADS = 8
AT_KV_HEADS = 2
AT_GROUP = AT_HEADS // AT_KV_HEADS
AT_KV_WIDTH = AT_KV_HEADS * AT_HEAD_DIM
WINDOW = 128
ROPE_THETA = 10000.0
EPS = 1e-6

COL_HQ = 0
COL_HF_FWD = COL_HQ + HG_WIDTH
COL_HF_BWD = COL_HF_FWD + HG_WIDTH
COL_HI = COL_HF_BWD + HG_WIDTH
COL_HG = COL_HI + HG_WIDTH
COL_AQ = COL_HG + HG_WIDTH
COL_AKV = COL_AQ + AT_WIDTH
N_IN = COL_AKV + 2 * AT_KV_WIDTH

V7X_LANES = 128
V7X_MXU_COLS = 256
V7X_VMEM_LIMIT = 58 * 1024 * 1024

TOKEN_TILE = 512
FF_CHUNK = V7X_MXU_COLS
SCAN_CHUNK = 128
SCAN_LEVELS = int(math.log2(SCAN_CHUNK))
ATT_BLOCK = WINDOW
ATT_STEP_BLOCKS = 2

BF16 = jnp.bfloat16
F32 = jnp.float32


def _dot(a, b):
    return jnp.dot(a, b, preferred_element_type=F32)


def _dot_nt(a, b):
    return lax.dot_general(a, b, (((1,), (1,)), ((), ())), preferred_element_type=F32)


def _dot_tn(a, b):
    return lax.dot_general(a, b, (((0,), (0,)), ((), ())), preferred_element_type=F32)


def _rmsnorm(x, gain):
    ms = jnp.mean(x * x, axis=-1, keepdims=True)
    return x * lax.rsqrt(ms + EPS) * gain


def _silu(x):
    return x * jax.nn.sigmoid(x)


def _swiglu_ffn(h, wi_ref, wo_ref):
    nchunks = D_FF // FF_CHUNK

    def gate_up(c):
        lo = c * FF_CHUNK
        return _dot(h, wi_ref[:, lo:lo + FF_CHUNK]), _dot(h, wi_ref[:, D_FF + lo:D_FF + lo + FF_CHUNK])

    acc = None
    ahead = gate_up(0)
    for c in range(nchunks):
        gate, up = ahead
        if c + 1 < nchunks:
            ahead = gate_up(c + 1)
        act = (_silu(gate) * up).astype(BF16)
        part = _dot(act, wo_ref[c * FF_CHUNK:(c + 1) * FF_CHUNK, :])
        acc = part if acc is None else acc + part
    return acc


def _log2_forget(fpre, lb_ref):
    a0 = lb_ref[0:1, :]
    a1 = lb_ref[1:2, :]
    m = jnp.maximum(a0, a1)
    e0 = jnp.exp(a0 - m)
    e1 = jnp.exp(a1 - m)
    lb = e0 / (e0 + e1)
    return jnp.log(lb + (1.0 - lb) * jax.nn.sigmoid(fpre)) * (1.0 / math.log(2.0))


def _head_mean_square(x, seg_ref):
    seg = seg_ref[...]
    outs = []
    for j in range(x.shape[1] // V7X_LANES):
        xs = x[:, j * V7X_LANES:(j + 1) * V7X_LANES]
        sq = xs * xs
        hi = sq.astype(BF16)
        lo = (sq - hi.astype(F32)).astype(BF16)
        outs.append(_dot(jnp.concatenate([hi, lo], axis=1), seg))
    return outs[0] if len(outs) == 1 else jnp.concatenate(outs, axis=1)


def _rope(x, cos, sin_signed):
    lane = lax.broadcasted_iota(jnp.int32, cos.shape, 1)
    first_half = (lane & (AT_HEAD_DIM // 2)) == 0
    outs = []
    for j in range(x.shape[1] // V7X_LANES):
        xs = x[:, j * V7X_LANES:(j + 1) * V7X_LANES]
        partner = jnp.where(first_half,
                            pltpu.roll(xs, V7X_LANES - AT_HEAD_DIM // 2, axis=1),
                            pltpu.roll(xs, AT_HEAD_DIM // 2, axis=1))
        outs.append(xs * cos + partner * sin_signed)
    return outs[0] if len(outs) == 1 else jnp.concatenate(outs, axis=1)


def _qk_norm_rope(x, mean_square, gain, cos, sin_signed):
    return _rope(x * lax.rsqrt(mean_square + EPS) * gain, cos, sin_signed)


def _ffn_in_kernel(x_ref, n1_ref, wi_ref, wo_ref, n2_ref, win_ref, lbf_ref, lbb_ref, qn_ref, kn_ref,
                   cos_ref, sin_ref, seg_ref,
                   x1_ref, q_ref, lff_ref, lfb_ref, v_ref, g_ref, aq_ref, akv_ref):
    x = x_ref[...]
    h = _rmsnorm(x, n1_ref[...]).astype(BF16)
    x1 = x + 0.5 * _swiglu_ffn(h, wi_ref, wo_ref)
    x1_ref[...] = x1
    h2 = _rmsnorm(x1, n2_ref[...]).astype(BF16)

    def proj(lo, width):
        return _dot(h2, win_ref[:, lo:lo + width])

    aq = proj(COL_AQ, AT_WIDTH)
    akv = proj(COL_AKV, 2 * AT_KV_WIDTH)
    lff_ref[...] = _log2_forget(proj(COL_HF_FWD, HG_WIDTH), lbf_ref)
    lfb_ref[...] = _log2_forget(proj(COL_HF_BWD, HG_WIDTH), lbb_ref)
    aq_ms = _head_mean_square(aq, seg_ref)
    ak_ms = _head_mean_square(akv[:, :AT_KV_WIDTH], seg_ref)
    q_ref[...] = _silu(proj(COL_HQ, HG_WIDTH))
    g_ref[...] = _silu(proj(COL_HG, HG_WIDTH))
    v_ref[...] = proj(COL_HI, HG_WIDTH).astype(BF16)
    cos = cos_ref[...]
    sin = sin_ref[...]
    aq = _qk_norm_rope(aq, aq_ms, qn_ref[...], cos, sin) * (1.0 / math.sqrt(AT_HEAD_DIM))
    aq_ref[...] = aq.astype(BF16)
    ak = _qk_norm_rope(akv[:, :AT_KV_WIDTH], ak_ms, kn_ref[...], cos, sin)
    av = akv[:, AT_KV_WIDTH:]
    akv_ref[...] = jnp.concatenate(
        [ak, pltpu.roll(ak, AT_HEAD_DIM, axis=1), av, pltpu.roll(av, AT_HEAD_DIM, axis=1)], axis=1).astype(BF16)


SUBLANES = 8
GROUP_LEVELS = 3


def _cumsum_rows(lf, tri_ref):
    hi = lf.astype(BF16)
    lo = (lf - hi.astype(F32)).astype(BF16)
    tri = tri_ref[...]
    return _dot(tri, hi) + _dot(tri, lo)


def _groups(a):
    return [a[g * SUBLANES:(g + 1) * SUBLANES, :] for g in range(a.shape[0] // SUBLANES)]


def _is_target_group(g, level, reverse):
    bit = (g >> (level - GROUP_LEVELS)) & 1
    return bit == (0 if reverse else 1)


def _gla_intra(q, lf, x, lvl, reverse):
    n = q.shape[0]
    ngroups = n // SUBLANES
    f = jnp.exp2(lf)
    k = 1.0 - f
    row = lax.broadcasted_iota(jnp.int32, q.shape, 0)
    lvl_g = _groups(lvl)
    a_g = [None] * ngroups

    def place(g, level, p_rows):
        keep = jnp.zeros_like(p_rows) if a_g[g] is None else a_g[g]
        a_g[g] = jnp.where(lvl_g[g] == level, p_rows, keep)

    for level in range(GROUP_LEVELS, SCAN_LEVELS):
        half = 1 << level
        q_rows, k_rows, targets = [], [], []
        for b in range(0, n, 2 * half):
            first, second = slice(b, b + half), slice(b + half, b + 2 * half)
            src, tgt = (second, first) if reverse else (first, second)
            r = b + half if reverse else b + half - 1
            x_r = x[r:r + 1, :]
            q_rows.append(q[tgt] * jnp.exp2(x[tgt] - x_r))
            k_src = k[src] * jnp.exp2(x_r - x[src])
            k_rows += [jnp.zeros_like(k_src), k_src] if reverse else [k_src, jnp.zeros_like(k_src)]
            targets += [g for g in range(b // SUBLANES, (b + 2 * half) // SUBLANES)
                        if _is_target_group(g, level, reverse)]
        p = _dot_nt(jnp.concatenate(q_rows, axis=0).astype(BF16), jnp.concatenate(k_rows, axis=0).astype(BF16))
        for i, g in enumerate(targets):
            place(g, level, p[i * SUBLANES:(i + 1) * SUBLANES, :])

    up = pltpu.roll(lf, n - 1, axis=0)
    down = pltpu.roll(lf, 1, axis=0)
    pos4 = row & 3
    if reverse:
        e1 = jnp.exp2(jnp.where(pos4 == 0, lf + up, jnp.where(pos4 == 1, lf, jnp.where(pos4 == 2, 0.0, down))))
        e0 = jnp.where((row & 1) == 0, f, 1.0)
        r2 = SUBLANES // 2
    else:
        e1 = jnp.exp2(jnp.where(pos4 == 0, up, jnp.where(pos4 == 1, 0.0, jnp.where(pos4 == 2, lf, lf + down))))
        e0 = jnp.where((row & 1) == 1, f, 1.0)
        r2 = SUBLANES // 2 - 1
    x_r2 = jnp.concatenate([jnp.broadcast_to(xg[r2:r2 + 1, :], xg.shape) for xg in _groups(x)], axis=0)
    e2 = jnp.exp2(-jnp.abs(x - x_r2))
    for level, e in enumerate((e0, e1, e2)):
        p = _dot_nt((q * e).astype(BF16), (k * e).astype(BF16))
        for g in range(ngroups):
            place(g, level, p[g * SUBLANES:(g + 1) * SUBLANES, :])

    a = jnp.concatenate(a_g, axis=0).astype(BF16)
    diag = jnp.sum(q * k, axis=-1, keepdims=True)
    x_end = x[0:1, :] if reverse else x[n - 1:n, :]
    q_in = (q * jnp.exp2(x)).astype(BF16)
    k_end = (k * jnp.exp2(x_end - x)).astype(BF16)
    return a, diag, q_in, k_end, jnp.exp2(x_end)


def _gla_inter(intra, v, state_ref):
    a, diag, q_in, k_end, decay = intra
    state = state_ref[...]
    o = _dot_nt(q_in, state.astype(BF16)) + _dot(a, v) + diag * v.astype(F32)
    state_ref[...] = state * decay + _dot_tn(v, k_end)
    return o


def _scan_kernel(lvl_f_ref, lvl_b_ref, tri_f_ref, tri_b_ref, qf_ref, lff_ref, vf_ref, qb_ref, lfb_ref, vb_ref,
                 of_ref, ob_ref, state_ref):
    @pl.when(pl.program_id(1) == 0)
    def _():
        state_ref[...] = jnp.zeros(state_ref.shape, F32)

    directions = ((qf_ref, lff_ref, vf_ref, of_ref, lvl_f_ref, tri_f_ref, False),
                  (qb_ref, lfb_ref, vb_ref, ob_ref, lvl_b_ref, tri_b_ref, True))
    heads = [(d, h) for d in range(2) for h in range(HG_HEADS)]
    cols = lambda h: slice(h * HG_DK, (h + 1) * HG_DK)
    cum = [_cumsum_rows(lf_ref[...], tri_ref) for _, lf_ref, _, _, _, tri_ref, _ in directions]
    intra = []
    for d, h in heads:
        q_ref, lf_ref, _, _, lvl_ref, _, reverse = directions[d]
        intra.append(_gla_intra(q_ref[:, cols(h)], lf_ref[:, cols(h)], cum[d][:, cols(h)], lvl_ref[...], reverse))
    for (d, h), part in zip(heads, intra):
        v_ref, o_ref = directions[d][2], directions[d][3]
        o_ref[:, cols(h)] = _gla_inter(part, v_ref[:, cols(h)], state_ref.at[d * HG_HEADS + h])


def _attn_kernel(sink_ref, band_ref, q_ref, kvp_ref, kvc_ref, kvn_ref, o_ref):
    i = pl.program_id(1)
    blk = ATT_BLOCK
    nsub = ATT_STEP_BLOCKS
    neg_inf = jnp.float32(-jnp.inf)
    band_prev = band_ref[:, 0:blk]
    band_next = band_ref[:, blk:2 * blk]
    no_prev = jnp.where(i == 0, neg_inf, 0.0)
    no_next = jnp.where(i == pl.num_programs(1) - 1, neg_inf, 0.0)
    lane = lax.broadcasted_iota(jnp.int32, (blk, V7X_LANES), 1)
    low = lane < AT_HEAD_DIM

    def kv_cat(col):
        sl = slice(col * V7X_LANES, (col + 1) * V7X_LANES)
        return jnp.concatenate([kvp_ref[:, sl], kvc_ref[:, sl], kvn_ref[:, sl]], axis=0)

    keys = (kv_cat(0), kv_cat(1))
    vals = (kv_cat(2), kv_cat(3))

    def swapped(hd):
        return int(hd % 2 != hd // AT_GROUP)

    def scores(j, hd):
        q_pair = q_ref[j * blk:(j + 1) * blk, (hd // 2) * V7X_LANES:(hd // 2 + 1) * V7X_LANES]
        q = jnp.where(low if hd % 2 == 0 else ~low, q_pair, jnp.zeros_like(q_pair))
        return _dot_nt(q, keys[swapped(hd)][j * blk:(j + 3) * blk])

    all_scores = [[scores(j, hd) for hd in range(AT_HEADS)] for j in range(nsub)]
    for j in range(nsub):
        bias_prev = band_prev + no_prev if j == 0 else band_prev
        bias_next = band_next + no_next if j == nsub - 1 else band_next
        for pair in range(AT_HEADS // 2):
            res = []
            for side in range(2):
                hd = 2 * pair + side
                s = all_scores[j][hd]
                s_prev = s[:, 0:blk] + bias_prev
                s_cur = s[:, blk:2 * blk]
                s_next = s[:, 2 * blk:3 * blk] + bias_next
                sink = sink_ref[hd]
                m = jnp.max(jnp.maximum(jnp.maximum(s_prev, s_cur), s_next), axis=-1, keepdims=True)
                m = jnp.maximum(m, sink)
                p_prev = jnp.exp(s_prev - m)
                p_cur = jnp.exp(s_cur - m)
                p_next = jnp.exp(s_next - m)
                denom = jnp.sum(p_prev + p_cur + p_next, axis=-1, keepdims=True) + jnp.exp(sink - m)
                p = jnp.concatenate([p_prev, p_cur, p_next], axis=1).astype(BF16)
                res.append(_dot(p, vals[swapped(hd)][j * blk:(j + 3) * blk]) / denom)
            o_ref[j * blk:(j + 1) * blk, pair * V7X_LANES:(pair + 1) * V7X_LANES] = (
                jnp.where(low, res[0], res[1]).astype(BF16))


def _ffn_out_kernel(x1_ref, of_ref, ob_ref, g_ref, oat_ref, onorm_ref, wout_ref, n3_ref, wi_ref, wo_ref,
                    y_ref):
    o = of_ref[...] + ob_ref[...]
    g = g_ref[...]
    onorm = onorm_ref[...]
    mixed = []
    for h in range(HG_HEADS):
        sl = slice(h * HG_DK, (h + 1) * HG_DK)
        mixed.append((_rmsnorm(o[:, sl], onorm) * g[:, sl]).astype(BF16))
    mixed.append(oat_ref[...])
    x2 = x1_ref[...] + _dot(jnp.concatenate(mixed, axis=1), wout_ref[...])
    h2 = _rmsnorm(x2, n3_ref[...]).astype(BF16)
    y_ref[...] = x2 + 0.5 * _swiglu_ffn(h2, wi_ref, wo_ref)


def _resident(shape):
    return pl.BlockSpec(shape, lambda *_: (0,) * len(shape), pipeline_mode=pl.Buffered(1))


def _params(semantics):
    return pltpu.CompilerParams(dimension_semantics=semantics, vmem_limit_bytes=V7X_VMEM_LIMIT)


def _scan_level_table(reverse):
    t = np.arange(SCAN_CHUNK)[:, None]
    s = np.arange(SCAN_CHUNK)[None, :]
    x = t ^ s
    lvl = np.where(x > 0, np.floor(np.log2(np.maximum(x, 1))).astype(np.int32), -1)
    valid = (s > t) if reverse else (s < t)
    return jnp.asarray(np.where(valid, lvl, -1).astype(np.int32))


def _band_bias():
    r = np.arange(ATT_BLOCK)[:, None]
    c = np.arange(ATT_BLOCK)[None, :]
    prev = np.where(c >= r, 0.0, -np.inf)
    nxt = np.where(c <= r, 0.0, -np.inf)
    return jnp.asarray(np.concatenate([prev, nxt], axis=1).astype(np.float32))


def _rope_tables(seq):
    pos = jnp.arange(seq, dtype=F32)
    inv_freq = ROPE_THETA ** (-jnp.arange(0, AT_HEAD_DIM, 2, dtype=F32) / AT_HEAD_DIM)
    ang = pos[:, None] * inv_freq[None, :]
    cos, sin = jnp.cos(ang), jnp.sin(ang)
    reps = V7X_LANES // AT_HEAD_DIM
    return jnp.tile(cos, (1, 2 * reps)), jnp.tile(jnp.concatenate([-sin, sin], axis=1), (1, reps))


def _layer(x, w):
    batch, seq, _ = x.shape
    tokens = batch * seq
    tm = TOKEN_TILE
    x2d = x.reshape(tokens, D_MODEL)
    tiles_per_seq = seq // tm

    row = lambda width: pl.BlockSpec((tm, width), lambda i: (i, 0))
    pos = lambda width: pl.BlockSpec((tm, width), lambda i: (i % tiles_per_seq, 0))
    sds = lambda width, dt: jax.ShapeDtypeStruct((tokens, width), dt)

    x1, q, lff, lfb, v, g, aq, akv = pl.pallas_call(
        _ffn_in_kernel,
        name="ffn_in",
        grid=(tokens // tm,),
        in_specs=[row(D_MODEL), _resident((1, D_MODEL)), _resident((D_MODEL, 2 * D_FF)),
                  _resident((D_FF, D_MODEL)), _resident((1, D_MODEL)), _resident((D_MODEL, N_IN)),
                  _resident((2, HG_WIDTH)), _resident((2, HG_WIDTH)), _resident((1, AT_WIDTH)),
                  _resident((1, AT_KV_WIDTH)), pos(V7X_LANES), pos(V7X_LANES),
                  _resident((2 * V7X_LANES, V7X_LANES))],
        out_specs=[row(D_MODEL), row(HG_WIDTH), row(HG_WIDTH), row(HG_WIDTH), row(HG_WIDTH), row(HG_WIDTH),
                   row(AT_WIDTH), row(4 * AT_KV_WIDTH)],
        out_shape=[sds(D_MODEL, F32), sds(HG_WIDTH, F32), sds(HG_WIDTH, F32), sds(HG_WIDTH, F32),
                   sds(HG_WIDTH, BF16), sds(HG_WIDTH, F32), sds(AT_WIDTH, BF16), sds(4 * AT_KV_WIDTH, BF16)],
        compiler_params=_params(("parallel",)),
    )(x2d, w["ffn1_norm"], w["ffn1_wi"], w["ffn1_wo"], w["mix_norm"], w["w_in"], w["hg_lb_fwd"],
      w["hg_lb_bwd"], w["q_norm"], w["k_norm"], w["cos"], w["sin"], w["seg"])

    c = SCAN_CHUNK
    nc = seq // c
    fwd = lambda b, i: (b * nc + i, 0)
    bwd = lambda b, i: (b * nc + nc - 1 - i, 0)
    chunk = lambda imap: pl.BlockSpec((c, HG_WIDTH), imap)
    o_f, o_b = pl.pallas_call(
        _scan_kernel,
        name="hgrn_scan",
        grid=(batch, nc),
        in_specs=[_resident((c, c)), _resident((c, c)), _resident((c, c)), _resident((c, c)),
                  chunk(fwd), chunk(fwd), chunk(fwd), chunk(bwd), chunk(bwd), chunk(bwd)],
        out_specs=[chunk(fwd), chunk(bwd)],
        out_shape=[sds(HG_WIDTH, F32), sds(HG_WIDTH, F32)],
        scratch_shapes=[pltpu.VMEM((2 * HG_HEADS, HG_DK, HG_DK), F32)],
        compiler_params=_params(("parallel", "arbitrary")),
    )(w["lvl_fwd"], w["lvl_bwd"], w["tri_fwd"], w["tri_bwd"], q, lff, v, q, lfb, v)

    blk = ATT_BLOCK
    nsub = ATT_STEP_BLOCKS
    nb = seq // blk
    ns = nb // nsub
    cur = lambda b, i: (b * ns + i, 0)
    prv = lambda b, i: (b * nb + jnp.maximum(nsub * i - 1, 0), 0)
    nxt = lambda b, i: (b * nb + jnp.minimum(nsub * i + nsub, nb - 1), 0)
    kv = lambda rows, imap: pl.BlockSpec((rows, 4 * AT_KV_WIDTH), imap)
    o_at = pl.pallas_call(
        _attn_kernel,
        name="win_attn",
        grid=(batch, ns),
        in_specs=[pl.BlockSpec(memory_space=pltpu.SMEM), _resident((blk, 2 * blk)),
                  pl.BlockSpec((nsub * blk, AT_WIDTH), cur), kv(blk, prv), kv(nsub * blk, cur), kv(blk, nxt)],
        out_specs=pl.BlockSpec((nsub * blk, AT_WIDTH), cur),
        out_shape=sds(AT_WIDTH, BF16),
        compiler_params=_params(("parallel", "parallel")),
    )(w["attn_sink"], w["band"], aq, akv, akv, akv)

    y = pl.pallas_call(
        _ffn_out_kernel,
        name="ffn_out",
        grid=(tokens // tm,),
        in_specs=[row(D_MODEL), row(HG_WIDTH), row(HG_WIDTH), row(HG_WIDTH), row(AT_WIDTH),
                  _resident((1, HG_DK)), _resident((D_MODEL, D_MODEL)), _resident((1, D_MODEL)),
                  _resident((D_MODEL, 2 * D_FF)), _resident((D_FF, D_MODEL))],
        out_specs=row(D_MODEL),
        out_shape=sds(D_MODEL, F32),
        compiler_params=_params(("parallel",)),
    )(x1, o_f, o_b, g, o_at, w["hg_out_norm"], w["w_out"], w["ffn2_norm"], w["ffn2_wi"], w["ffn2_wo"])
    return y.reshape(batch, seq, D_MODEL)


def kernel(x_prompt, x_sample, ffn1_norm, ffn1_wi, ffn1_wo, mix_norm, w_in, hg_lb_fwd, hg_lb_bwd, hg_out_norm,
           q_norm, k_norm, attn_sink, w_out, ffn2_norm, ffn2_wi, ffn2_wo):
    seq = x_prompt.shape[1]
    assert x_sample.shape[1] == seq and seq % TOKEN_TILE == 0
    cos, sin = _rope_tables(seq)
    heads_per_group = V7X_LANES // AT_HEAD_DIM
    seg = jnp.kron(jnp.eye(heads_per_group, dtype=F32), jnp.full((AT_HEAD_DIM, AT_HEAD_DIM), 1.0 / AT_HEAD_DIM, F32))
    seg = jnp.concatenate([seg, seg], axis=0)
    w = {
        "ffn1_norm": ffn1_norm[0][None, :], "ffn1_wi": ffn1_wi[0].astype(BF16), "ffn1_wo": ffn1_wo[0].astype(BF16),
        "mix_norm": mix_norm[0][None, :], "w_in": w_in[0].astype(BF16),
        "hg_lb_fwd": hg_lb_fwd, "hg_lb_bwd": hg_lb_bwd, "hg_out_norm": hg_out_norm[0][None, :],
        "q_norm": jnp.tile(q_norm[0], AT_HEADS)[None, :], "k_norm": jnp.tile(k_norm[0], AT_KV_HEADS)[None, :],
        "attn_sink": attn_sink[0], "w_out": w_out[0].astype(BF16),
        "ffn2_norm": ffn2_norm[0][None, :], "ffn2_wi": ffn2_wi[0].astype(BF16), "ffn2_wo": ffn2_wo[0].astype(BF16),
        "cos": cos, "sin": sin, "seg": seg.astype(BF16),
        "lvl_fwd": _scan_level_table(False), "lvl_bwd": _scan_level_table(True), "band": _band_bias(),
        "tri_fwd": jnp.asarray(np.tril(np.ones((SCAN_CHUNK, SCAN_CHUNK), np.float32)), BF16),
        "tri_bwd": jnp.asarray(np.triu(np.ones((SCAN_CHUNK, SCAN_CHUNK), np.float32)), BF16),
    }
    return (_layer(x_prompt, w), _layer(x_sample, w))
```

```python
import functools
import math

import jax
import jax.numpy as jnp
import numpy as np
from jax import lax
from jax.experimental import pallas as pl
from jax.experimental.pallas import tpu as pltpu

D_MODEL = 1024
D_FF = 2816
HG_WIDTH = 512
HG_HEADS = 4
HG_DK = 128
AT_WIDTH = 512
AT_HEAD_DIM = 64
AT_HEADS = 8
AT_KV_HEADS = 2
AT_GROUP = AT_HEADS // AT_KV_HEADS
AT_KV_WIDTH = AT_KV_HEADS * AT_HEAD_DIM
WINDOW = 128
ROPE_THETA = 10000.0
EPS = 1e-6

COL_HQ = 0
COL_HF_FWD = COL_HQ + HG_WIDTH
COL_HF_BWD = COL_HF_FWD + HG_WIDTH
COL_HI = COL_HF_BWD + HG_WIDTH
COL_HG = COL_HI + HG_WIDTH
COL_AQ = COL_HG + HG_WIDTH
COL_AKV = COL_AQ + AT_WIDTH
N_IN = COL_AKV + 2 * AT_KV_WIDTH

V7X_LANES = 128
V7X_MXU_COLS = 256
V7X_VMEM_LIMIT = 58 * 1024 * 1024

TOKEN_TILE = 512
FF_CHUNK = V7X_MXU_COLS
SCAN_CHUNK = 128
SCAN_LEVELS = int(math.log2(SCAN_CHUNK))
ATT_BLOCK = WINDOW

BF16 = jnp.bfloat16
F32 = jnp.float32


def _dot(a, b):
    return jnp.dot(a, b, preferred_element_type=F32)


def _dot_nt(a, b):
    return lax.dot_general(a, b, (((1,), (1,)), ((), ())), preferred_element_type=F32)


def _dot_tn(a, b):
    return lax.dot_general(a, b, (((0,), (0,)), ((), ())), preferred_element_type=F32)


def _rmsnorm(x, gain):
    ms = jnp.mean(x * x, axis=-1, keepdims=True)
    return x * lax.rsqrt(ms + EPS) * gain


def _silu(x):
    return x * jax.nn.sigmoid(x)


def _run_interleaved(*stage_lists):
    order = []
    for k, stages in enumerate(stage_lists):
        total = float(sum(weight for weight, _ in stages))
        done = 0.0
        for i, (weight, _) in enumerate(stages):
            order.append(((done + 0.5 * weight) / total, k, i))
            done += weight
    for _, k, i in sorted(order):
        stage_lists[k][i][1]()


def _swiglu_stages(h, wi_ref, wo_ref, out):
    nchunks = D_FF // FF_CHUNK
    st = {}

    def gate_up(c):
        lo = c * FF_CHUNK
        return _dot(h, wi_ref[:, lo:lo + FF_CHUNK]), _dot(h, wi_ref[:, D_FF + lo:D_FF + lo + FF_CHUNK])

    def chunk(c):
        if c == 0:
            st["ahead"] = gate_up(0)
        gate, up = st["ahead"]
        if c + 1 < nchunks:
            st["ahead"] = gate_up(c + 1)
        act = (_silu(gate) * up).astype(BF16)
        part = _dot(act, wo_ref[c * FF_CHUNK:(c + 1) * FF_CHUNK, :])
        out["ffn"] = part if c == 0 else out["ffn"] + part

    return [(1, functools.partial(chunk, c)) for c in range(nchunks)]


def _swiglu_ffn(h, wi_ref, wo_ref):
    out = {}
    _run_interleaved(_swiglu_stages(h, wi_ref, wo_ref, out))
    return out["ffn"]


def _log2_forget(fpre, lb_ref):
    a0 = lb_ref[0:1, :]
    a1 = lb_ref[1:2, :]
    m = jnp.maximum(a0, a1)
    e0 = jnp.exp(a0 - m)
    e1 = jnp.exp(a1 - m)
    lb = e0 / (e0 + e1)
    return jnp.log(lb + (1.0 - lb) * jax.nn.sigmoid(fpre)) * (1.0 / math.log(2.0))


def _head_mean_square(x, seg_ref):
    seg = seg_ref[...]
    outs = []
    for j in range(x.shape[1] // V7X_LANES):
        xs = x[:, j * V7X_LANES:(j + 1) * V7X_LANES]
        sq = xs * xs
        hi = sq.astype(BF16)
        lo = (sq - hi.astype(F32)).astype(BF16)
        outs.append(_dot(jnp.concatenate([hi, lo], axis=1), seg))
    return outs[0] if len(outs) == 1 else jnp.concatenate(outs, axis=1)


def _rope(x, cos, sin_signed):
    lane = lax.broadcasted_iota(jnp.int32, cos.shape, 1)
    first_half = (lane & (AT_HEAD_DIM // 2)) == 0
    outs = []
    for j in range(x.shape[1] // V7X_LANES):
        xs = x[:, j * V7X_LANES:(j + 1) * V7X_LANES]
        partner = jnp.where(first_half,
                            pltpu.roll(xs, V7X_LANES - AT_HEAD_DIM // 2, axis=1),
                            pltpu.roll(xs, AT_HEAD_DIM // 2, axis=1))
        outs.append(xs * cos + partner * sin_signed)
    return outs[0] if len(outs) == 1 else jnp.concatenate(outs, axis=1)


def _qk_norm_rope(x, mean_square, gain, cos, sin_signed):
    return _rope(x * lax.rsqrt(mean_square + EPS) * gain, cos, sin_signed)


def _ffn_in_kernel(x_ref, n1_ref, wi_ref, wo_ref, n2_ref, win_ref, lbf_ref, lbb_ref, qn_ref, kn_ref,
                   cos_ref, sin_ref, seg_ref,
                   x1_ref, q_ref, lff_ref, lfb_ref, v_ref, g_ref, aq_ref, akv_ref):
    x = x_ref[...]
    h = _rmsnorm(x, n1_ref[...]).astype(BF16)
    x1 = x + 0.5 * _swiglu_ffn(h, wi_ref, wo_ref)
    x1_ref[...] = x1
    h2 = _rmsnorm(x1, n2_ref[...]).astype(BF16)

    def proj(lo, width):
        return _dot(h2, win_ref[:, lo:lo + width])

    aq = proj(COL_AQ, AT_WIDTH)
    akv = proj(COL_AKV, 2 * AT_KV_WIDTH)
    lff_ref[...] = _log2_forget(proj(COL_HF_FWD, HG_WIDTH), lbf_ref)
    lfb_ref[...] = _log2_forget(proj(COL_HF_BWD, HG_WIDTH), lbb_ref)
    aq_ms = _head_mean_square(aq, seg_ref)
    ak_ms = _head_mean_square(akv[:, :AT_KV_WIDTH], seg_ref)
    q_ref[...] = _silu(proj(COL_HQ, HG_WIDTH))
    g_ref[...] = _silu(proj(COL_HG, HG_WIDTH))
    v_ref[...] = proj(COL_HI, HG_WIDTH).astype(BF16)
    cos = cos_ref[...]
    sin = sin_ref[...]
    aq = _qk_norm_rope(aq, aq_ms, qn_ref[...], cos, sin) * (1.0 / math.sqrt(AT_HEAD_DIM))
    aq_ref[...] = aq.astype(BF16)
    ak = _qk_norm_rope(akv[:, :AT_KV_WIDTH], ak_ms, kn_ref[...], cos, sin)
    av = akv[:, AT_KV_WIDTH:]
    akv_ref[...] = jnp.concatenate(
        [ak, pltpu.roll(ak, AT_HEAD_DIM, axis=1), av, pltpu.roll(av, AT_HEAD_DIM, axis=1)], axis=1).astype(BF16)


SUBLANES = 8
GROUP_LEVELS = 3


def _cumsum_rows(lf, tri_ref):
    hi = lf.astype(BF16)
    lo = (lf - hi.astype(F32)).astype(BF16)
    tri = tri_ref[...]
    return _dot(tri, hi) + _dot(tri, lo)


def _groups(a):
    return [a[g * SUBLANES:(g + 1) * SUBLANES, :] for g in range(a.shape[0] // SUBLANES)]


def _is_target_group(g, level, reverse):
    bit = (g >> (level - GROUP_LEVELS)) & 1
    return bit == (0 if reverse else 1)


def _gla_intra(q, lf, x, lvl, reverse):
    n = q.shape[0]
    ngroups = n // SUBLANES
    f = jnp.exp2(lf)
    k = 1.0 - f
    row = lax.broadcasted_iota(jnp.int32, q.shape, 0)
    lvl_g = _groups(lvl)
    a_g = [None] * ngroups

    def place(g, level, p_rows):
        keep = jnp.zeros_like(p_rows) if a_g[g] is None else a_g[g]
        a_g[g] = jnp.where(lvl_g[g] == level, p_rows, keep)

    for level in range(GROUP_LEVELS, SCAN_LEVELS):
        half = 1 << level
        q_rows, k_rows, targets = [], [], []
        for b in range(0, n, 2 * half):
            first, second = slice(b, b + half), slice(b + half, b + 2 * half)
            src, tgt = (second, first) if reverse else (first, second)
            r = b + half if reverse else b + half - 1
            x_r = x[r:r + 1, :]
            q_rows.append(q[tgt] * jnp.exp2(x[tgt] - x_r))
            k_src = k[src] * jnp.exp2(x_r - x[src])
            k_rows += [jnp.zeros_like(k_src), k_src] if reverse else [k_src, jnp.zeros_like(k_src)]
            targets += [g for g in range(b // SUBLANES, (b + 2 * half) // SUBLANES)
                        if _is_target_group(g, level, reverse)]
        p = _dot_nt(jnp.concatenate(q_rows, axis=0).astype(BF16), jnp.concatenate(k_rows, axis=0).astype(BF16))
        for i, g in enumerate(targets):
            place(g, level, p[i * SUBLANES:(i + 1) * SUBLANES, :])

    up = pltpu.roll(lf, n - 1, axis=0)
    down = pltpu.roll(lf, 1, axis=0)
    pos4 = row & 3
    if reverse:
        e1 = jnp.exp2(jnp.where(pos4 == 0, lf + up, jnp.where(pos4 == 1, lf, jnp.where(pos4 == 2, 0.0, down))))
        e0 = jnp.where((row & 1) == 0, f, 1.0)
        r2 = SUBLANES // 2
    else:
        e1 = jnp.exp2(jnp.where(pos4 == 0, up, jnp.where(pos4 == 1, 0.0, jnp.where(pos4 == 2, lf, lf + down))))
        e0 = jnp.where((row & 1) == 1, f, 1.0)
        r2 = SUBLANES // 2 - 1
    x_r2 = jnp.concatenate([jnp.broadcast_to(xg[r2:r2 + 1, :], xg.shape) for xg in _groups(x)], axis=0)
    e2 = jnp.exp2(-jnp.abs(x - x_r2))
    for level, e in enumerate((e0, e1, e2)):
        p = _dot_nt((q * e).astype(BF16), (k * e).astype(BF16))
        for g in range(ngroups):
            place(g, level, p[g * SUBLANES:(g + 1) * SUBLANES, :])

    a = jnp.concatenate(a_g, axis=0).astype(BF16)
    diag = jnp.sum(q * k, axis=-1, keepdims=True)
    x_end = x[0:1, :] if reverse else x[n - 1:n, :]
    q_in = (q * jnp.exp2(x)).astype(BF16)
    k_end = (k * jnp.exp2(x_end - x)).astype(BF16)
    return a, diag, q_in, k_end, jnp.exp2(x_end)


def _gla_inter(intra, v, state_ref):
    a, diag, q_in, k_end, decay = intra
    state = state_ref[...]
    o = _dot_nt(q_in, state.astype(BF16)) + _dot(a, v) + diag * v.astype(F32)
    state_ref[...] = state * decay + _dot_tn(v, k_end)
    return o


def _scan_kernel(lvl_f_ref, lvl_b_ref, tri_f_ref, tri_b_ref, qf_ref, lff_ref, vf_ref, qb_ref, lfb_ref, vb_ref,
                 of_ref, ob_ref, state_ref):
    @pl.when(pl.program_id(1) == 0)
    def _():
        state_ref[...] = jnp.zeros(state_ref.shape, F32)

    directions = ((qf_ref, lff_ref, vf_ref, of_ref, lvl_f_ref, tri_f_ref, False),
                  (qb_ref, lfb_ref, vb_ref, ob_ref, lvl_b_ref, tri_b_ref, True))
    heads = [(d, h) for d in range(2) for h in range(HG_HEADS)]
    cols = lambda h: slice(h * HG_DK, (h + 1) * HG_DK)
    cum = [_cumsum_rows(lf_ref[...], tri_ref) for _, lf_ref, _, _, _, tri_ref, _ in directions]
    intra = []
    for d, h in heads:
        q_ref, lf_ref, _, _, lvl_ref, _, reverse = directions[d]
        intra.append(_gla_intra(q_ref[:, cols(h)], lf_ref[:, cols(h)], cum[d][:, cols(h)], lvl_ref[...], reverse))
    for (d, h), part in zip(heads, intra):
        v_ref, o_ref = directions[d][2], directions[d][3]
        o_ref[:, cols(h)] = _gla_inter(part, v_ref[:, cols(h)], state_ref.at[d * HG_HEADS + h])


ATTN_LAG = 4


def _attn_stages(sink_ref, band_ref, q_ref, kvp_ref, kvc_ref, kvn_ref, o_ref, first_in_row, last_in_row):
    blk = ATT_BLOCK
    nsub = q_ref.shape[0] // blk
    st = {}

    def swapped(hd):
        return int(hd % 2 != hd // AT_GROUP)

    def prepare():
        neg_inf = jnp.float32(-jnp.inf)
        st["band_prev"] = band_ref[:, 0:blk]
        st["band_next"] = band_ref[:, blk:2 * blk]
        st["no_prev"] = jnp.where(first_in_row, neg_inf, 0.0)
        st["no_next"] = jnp.where(last_in_row, neg_inf, 0.0)
        lane = lax.broadcasted_iota(jnp.int32, (blk, V7X_LANES), 1)
        st["low"] = lane < AT_HEAD_DIM

        def kv_cat(col):
            sl = slice(col * V7X_LANES, (col + 1) * V7X_LANES)
            return jnp.concatenate([kvp_ref[:, sl], kvc_ref[:, sl], kvn_ref[:, sl]], axis=0)

        st["keys"] = (kv_cat(0), kv_cat(1))
        st["vals"] = (kv_cat(2), kv_cat(3))

    def scores(j):
        out = []
        for hd in range(AT_HEADS):
            q_pair = q_ref[j * blk:(j + 1) * blk, (hd // 2) * V7X_LANES:(hd // 2 + 1) * V7X_LANES]
            q = jnp.where(st["low"] if hd % 2 == 0 else ~st["low"], q_pair, jnp.zeros_like(q_pair))
            out.append(_dot_nt(q, st["keys"][swapped(hd)][j * blk:(j + 3) * blk]))
        st["scores", j] = out

    def softmax(j, pair):
        bias_prev = st["band_prev"] + st["no_prev"] if j == 0 else st["band_prev"]
        bias_next = st["band_next"] + st["no_next"] if j == nsub - 1 else st["band_next"]
        for side in range(2):
            hd = 2 * pair + side
            s = st["scores", j][hd]
            s_prev = s[:, 0:blk] + bias_prev
            s_cur = s[:, blk:2 * blk]
            s_next = s[:, 2 * blk:3 * blk] + bias_next
            sink = sink_ref[hd]
            m = jnp.max(jnp.maximum(jnp.maximum(s_prev, s_cur), s_next), axis=-1, keepdims=True)
            m = jnp.maximum(m, sink)
            p_prev = jnp.exp(s_prev - m)
            p_cur = jnp.exp(s_cur - m)
            p_next = jnp.exp(s_next - m)
            denom = jnp.sum(p_prev + p_cur + p_next, axis=-1, keepdims=True) + jnp.exp(sink - m)
            st["p", j, hd] = (jnp.concatenate([p_prev, p_cur, p_next], axis=1).astype(BF16), denom)

    def weighted_values(j, pair):
        res = []
        for side in range(2):
            hd = 2 * pair + side
            p, denom = st["p", j, hd]
            res.append(_dot(p, st["vals"][swapped(hd)][j * blk:(j + 3) * blk]) / denom)
        o_ref[j * blk:(j + 1) * blk, pair * V7X_LANES:(pair + 1) * V7X_LANES] = (
            jnp.where(st["low"], res[0], res[1]).astype(BF16))

    npairs = AT_HEADS // 2
    work = [(j, pair) for j in range(nsub) for pair in range(npairs)]
    stages = [(1, prepare)]
    for i in range(len(work) + ATTN_LAG):
        fns, weight = [], 0
        if i >= ATTN_LAG:
            fns.append(functools.partial(weighted_values, *work[i - ATTN_LAG]))
            weight += 1
        if i < len(work):
            if work[i][1] == 0:
                fns.append(functools.partial(scores, work[i][0]))
            fns.append(functools.partial(softmax, *work[i]))
            weight += 4
        stages.append((weight, lambda fns=fns: [fn() for fn in fns]))
    return stages


def _ffn_out_kernel(x1_ref, of_ref, ob_ref, g_ref, onorm_ref, wout_ref, n3_ref, wi_ref, wo_ref,
                    sink_ref, band_ref, q0_ref, kvp0_ref, kvc0_ref, kvn0_ref, q_ref, kvp_ref, kvc_ref, kvn_ref,
                    y_ref, oat_ref, *, tiles_per_seq):
    s = pl.program_id(0)
    nxt = jnp.minimum(s + 1, pl.num_programs(0) - 1)
    pos = lax.rem(nxt, tiles_per_seq)

    @pl.when(s == 0)
    def _():
        _run_interleaved(_attn_stages(sink_ref, band_ref, q0_ref, kvp0_ref, kvc0_ref, kvn0_ref, oat_ref.at[0],
                                      True, tiles_per_seq == 1))

    slot = lax.rem(s, 2)
    st = {}

    def mix():
        o = of_ref[...] + ob_ref[...]
        g = g_ref[...]
        onorm = onorm_ref[...]
        mixed = []
        for h in range(HG_HEADS):
            sl = slice(h * HG_DK, (h + 1) * HG_DK)
            mixed.append((_rmsnorm(o[:, sl], onorm) * g[:, sl]).astype(BF16))
        mixed.append(oat_ref[slot])
        st["x2"] = x1_ref[...] + _dot(jnp.concatenate(mixed, axis=1), wout_ref[...])
        st["h2"] = _rmsnorm(st["x2"], n3_ref[...]).astype(BF16)

    def ffn_chunk(c):
        if c == 0:
            st["chunks"] = _swiglu_stages(st["h2"], wi_ref, wo_ref, st)
        st["chunks"][c][1]()

    def finish():
        y_ref[...] = st["x2"] + 0.5 * st["ffn"]

    dense = ([(1, mix)] + [(3, functools.partial(ffn_chunk, c)) for c in range(D_FF // FF_CHUNK)] + [(1, finish)])
    _run_interleaved(dense, _attn_stages(sink_ref, band_ref, q_ref, kvp_ref, kvc_ref, kvn_ref, oat_ref.at[1 - slot],
                                         pos == 0, pos == tiles_per_seq - 1))


def _resident(shape):
    return pl.BlockSpec(shape, lambda *_: (0,) * len(shape), pipeline_mode=pl.Buffered(1))


def _params(semantics):
    return pltpu.CompilerParams(dimension_semantics=semantics, vmem_limit_bytes=V7X_VMEM_LIMIT)


def _scan_level_table(reverse):
    t = np.arange(SCAN_CHUNK)[:, None]
    s = np.arange(SCAN_CHUNK)[None, :]
    x = t ^ s
    lvl = np.where(x > 0, np.floor(np.log2(np.maximum(x, 1))).astype(np.int32), -1)
    valid = (s > t) if reverse else (s < t)
    return jnp.asarray(np.where(valid, lvl, -1).astype(np.int32))


def _band_bias():
    r = np.arange(ATT_BLOCK)[:, None]
    c = np.arange(ATT_BLOCK)[None, :]
    prev = np.where(c >= r, 0.0, -np.inf)
    nxt = np.where(c <= r, 0.0, -np.inf)
    return jnp.asarray(np.concatenate([prev, nxt], axis=1).astype(np.float32))


def _rope_tables(seq):
    pos = jnp.arange(seq, dtype=F32)
    inv_freq = ROPE_THETA ** (-jnp.arange(0, AT_HEAD_DIM, 2, dtype=F32) / AT_HEAD_DIM)
    ang = pos[:, None] * inv_freq[None, :]
    cos, sin = jnp.cos(ang), jnp.sin(ang)
    reps = V7X_LANES // AT_HEAD_DIM
    return jnp.tile(cos, (1, 2 * reps)), jnp.tile(jnp.concatenate([-sin, sin], axis=1), (1, reps))


def _layer(x, w):
    batch, seq, _ = x.shape
    tokens = batch * seq
    tm = TOKEN_TILE
    x2d = x.reshape(tokens, D_MODEL)
    tiles_per_seq = seq // tm

    row = lambda width: pl.BlockSpec((tm, width), lambda i: (i, 0))
    pos = lambda width: pl.BlockSpec((tm, width), lambda i: (i % tiles_per_seq, 0))
    sds = lambda width, dt: jax.ShapeDtypeStruct((tokens, width), dt)

    x1, q, lff, lfb, v, g, aq, akv = pl.pallas_call(
        _ffn_in_kernel,
        name="ffn_in",
        grid=(tokens // tm,),
        in_specs=[row(D_MODEL), _resident((1, D_MODEL)), _resident((D_MODEL, 2 * D_FF)),
                  _resident((D_FF, D_MODEL)), _resident((1, D_MODEL)), _resident((D_MODEL, N_IN)),
                  _resident((2, HG_WIDTH)), _resident((2, HG_WIDTH)), _resident((1, AT_WIDTH)),
                  _resident((1, AT_KV_WIDTH)), pos(V7X_LANES), pos(V7X_LANES),
                  _resident((2 * V7X_LANES, V7X_LANES))],
        out_specs=[row(D_MODEL), row(HG_WIDTH), row(HG_WIDTH), row(HG_WIDTH), row(HG_WIDTH), row(HG_WIDTH),
                   row(AT_WIDTH), row(4 * AT_KV_WIDTH)],
        out_shape=[sds(D_MODEL, F32), sds(HG_WIDTH, F32), sds(HG_WIDTH, F32), sds(HG_WIDTH, F32),
                   sds(HG_WIDTH, BF16), sds(HG_WIDTH, F32), sds(AT_WIDTH, BF16), sds(4 * AT_KV_WIDTH, BF16)],
        compiler_params=_params(("parallel",)),
    )(x2d, w["ffn1_norm"], w["ffn1_wi"], w["ffn1_wo"], w["mix_norm"], w["w_in"], w["hg_lb_fwd"],
      w["hg_lb_bwd"], w["q_norm"], w["k_norm"], w["cos"], w["sin"], w["seg"])

    c = SCAN_CHUNK
    nc = seq // c
    fwd = lambda b, i: (b * nc + i, 0)
    bwd = lambda b, i: (b * nc + nc - 1 - i, 0)
    chunk = lambda imap: pl.BlockSpec((c, HG_WIDTH), imap)
    o_f, o_b = pl.pallas_call(
        _scan_kernel,
        name="hgrn_scan",
        grid=(batch, nc),
        in_specs=[_resident((c, c)), _resident((c, c)), _resident((c, c)), _resident((c, c)),
                  chunk(fwd), chunk(fwd), chunk(fwd), chunk(bwd), chunk(bwd), chunk(bwd)],
        out_specs=[chunk(fwd), chunk(bwd)],
        out_shape=[sds(HG_WIDTH, F32), sds(HG_WIDTH, F32)],
        scratch_shapes=[pltpu.VMEM((2 * HG_HEADS, HG_DK, HG_DK), F32)],
        compiler_params=_params(("parallel", "arbitrary")),
    )(w["lvl_fwd"], w["lvl_bwd"], w["tri_fwd"], w["tri_bwd"], q, lff, v, q, lfb, v)

    blk = ATT_BLOCK
    per_tile = tm // blk
    ntiles = tokens // tm
    nblocks = tokens // blk
    ahead = lambda i: jnp.minimum(i + 1, ntiles - 1)
    kv_width = 4 * AT_KV_WIDTH
    before = lambda t: jnp.maximum(per_tile * t - 1, 0)
    after = lambda t: jnp.minimum(per_tile * t + per_tile, nblocks - 1)
    y = pl.pallas_call(
        functools.partial(_ffn_out_kernel, tiles_per_seq=tiles_per_seq),
        name="ffn_out",
        grid=(ntiles,),
        in_specs=[row(D_MODEL), row(HG_WIDTH), row(HG_WIDTH), row(HG_WIDTH),
                  _resident((1, HG_DK)), _resident((D_MODEL, D_MODEL)), _resident((1, D_MODEL)),
                  _resident((D_MODEL, 2 * D_FF)), _resident((D_FF, D_MODEL)),
                  pl.BlockSpec(memory_space=pltpu.SMEM), _resident((blk, 2 * blk)),
                  pl.BlockSpec((tm, AT_WIDTH), lambda i: (0, 0)),
                  pl.BlockSpec((blk, kv_width), lambda i: (0, 0)),
                  pl.BlockSpec((tm, kv_width), lambda i: (0, 0)),
                  pl.BlockSpec((blk, kv_width), lambda i: (min(per_tile, nblocks - 1), 0)),
                  pl.BlockSpec((tm, AT_WIDTH), lambda i: (ahead(i), 0)),
                  pl.BlockSpec((blk, kv_width), lambda i: (before(ahead(i)), 0)),
                  pl.BlockSpec((tm, kv_width), lambda i: (ahead(i), 0)),
                  pl.BlockSpec((blk, kv_width), lambda i: (after(ahead(i)), 0))],
        out_specs=row(D_MODEL),
        out_shape=sds(D_MODEL, F32),
        scratch_shapes=[pltpu.VMEM((2, tm, AT_WIDTH), BF16)],
        compiler_params=_params(("arbitrary",)),
    )(x1, o_f, o_b, g, w["hg_out_norm"], w["w_out"], w["ffn2_norm"], w["ffn2_wi"], w["ffn2_wo"],
      w["attn_sink"], w["band"], aq, akv, akv, akv, aq, akv, akv, akv)
    return y.reshape(batch, seq, D_MODEL)


def kernel(x_prompt, x_sample, ffn1_norm, ffn1_wi, ffn1_wo, mix_norm, w_in, hg_lb_fwd, hg_lb_bwd, hg_out_norm,
           q_norm, k_norm, attn_sink, w_out, ffn2_norm, ffn2_wi, ffn2_wo):
    seq = x_prompt.shape[1]
    assert x_sample.shape[1] == seq and seq % TOKEN_TILE == 0
    cos, sin = _rope_tables(seq)
    heads_per_group = V7X_LANES // AT_HEAD_DIM
    seg = jnp.kron(jnp.eye(heads_per_group, dtype=F32), jnp.full((AT_HEAD_DIM, AT_HEAD_DIM), 1.0 / AT_HEAD_DIM, F32))
    seg = jnp.concatenate([seg, seg], axis=0)
    w = {
        "ffn1_norm": ffn1_norm[0][None, :], "ffn1_wi": ffn1_wi[0].astype(BF16), "ffn1_wo": ffn1_wo[0].astype(BF16),
        "mix_norm": mix_norm[0][None, :], "w_in": w_in[0].astype(BF16),
        "hg_lb_fwd": hg_lb_fwd, "hg_lb_bwd": hg_lb_bwd, "hg_out_norm": hg_out_norm[0][None, :],
        "q_norm": jnp.tile(q_norm[0], AT_HEADS)[None, :], "k_norm": jnp.tile(k_norm[0], AT_KV_HEADS)[None, :],
        "attn_sink": attn_sink[0], "w_out": w_out[0].astype(BF16),
        "ffn2_norm": ffn2_norm[0][None, :], "ffn2_wi": ffn2_wi[0].astype(BF16), "ffn2_wo": ffn2_wo[0].astype(BF16),
        "cos": cos, "sin": sin, "seg": seg.astype(BF16),
        "lvl_fwd": _scan_level_table(False), "lvl_bwd": _scan_level_table(True), "band": _band_bias(),
        "tri_fwd": jnp.asarray(np.tril(np.ones((SCAN_CHUNK, SCAN_CHUNK), np.float32)), BF16),
        "tri_bwd": jnp.asarray(np.triu(np.ones((SCAN_CHUNK, SCAN_CHUNK), np.float32)), BF16),
    }
    return (_layer(x_prompt, w), _layer(x_sample, w))
```

```python
import functools
import math

import jax
import jax.numpy as jnp
import numpy as np
from jax import lax
from jax.experimental import pallas as pl
from jax.experimental.pallas import tpu as pltpu

D_MODEL = 1024
D_FF = 2816
HG_WIDTH = 512
HG_HEADS = 4
HG_DK = 128
AT_WIDTH = 512
AT_HEAD_DIM = 64
AT_HEADS = 8
AT_KV_HEADS = 2
AT_GROUP = AT_HEADS // AT_KV_HEADS
AT_KV_WIDTH = AT_KV_HEADS * AT_HEAD_DIM
WINDOW = 128
ROPE_THETA = 10000.0
EPS = 1e-6

COL_HQ = 0
COL_HF_FWD = COL_HQ + HG_WIDTH
COL_HF_BWD = COL_HF_FWD + HG_WIDTH
COL_HI = COL_HF_BWD + HG_WIDTH
COL_HG = COL_HI + HG_WIDTH
COL_AQ = COL_HG + HG_WIDTH
COL_AKV = COL_AQ + AT_WIDTH
N_IN = COL_AKV + 2 * AT_KV_WIDTH

V7X_LANES = 128
V7X_MXU_COLS = 256
V7X_VMEM_LIMIT = 58 * 1024 * 1024

TOKEN_TILE = 512
FF_CHUNK = V7X_MXU_COLS
SCAN_CHUNK = 128
SCAN_LEVELS = int(math.log2(SCAN_CHUNK))
SCAN_BLOCK = 4 * SCAN_CHUNK
ATT_BLOCK = WINDOW

BF16 = jnp.bfloat16
F32 = jnp.float32


def _dot(a, b):
    return jnp.dot(a, b, preferred_element_type=F32)


def _dot_nt(a, b):
    return lax.dot_general(a, b, (((1,), (1,)), ((), ())), preferred_element_type=F32)


def _dot_tn(a, b):
    return lax.dot_general(a, b, (((0,), (0,)), ((), ())), preferred_element_type=F32)


def _rmsnorm(x, gain):
    ms = jnp.mean(x * x, axis=-1, keepdims=True)
    return x * lax.rsqrt(ms + EPS) * gain


def _silu(x):
    return x * jax.nn.sigmoid(x)


def _run_interleaved(*stage_lists):
    order = []
    for k, stages in enumerate(stage_lists):
        total = float(sum(weight for weight, _ in stages))
        done = 0.0
        for i, (weight, _) in enumerate(stages):
            order.append(((done + 0.5 * weight) / total, k, i))
            done += weight
    for _, k, i in sorted(order):
        stage_lists[k][i][1]()


def _swiglu_stages(h, wi_ref, wo_ref, out):
    nchunks = D_FF // FF_CHUNK
    st = {}

    def gate_up(c):
        lo = c * FF_CHUNK
        return _dot(h, wi_ref[:, lo:lo + FF_CHUNK]), _dot(h, wi_ref[:, D_FF + lo:D_FF + lo + FF_CHUNK])

    def chunk(c):
        if c == 0:
            st["ahead"] = gate_up(0)
        gate, up = st["ahead"]
        if c + 1 < nchunks:
            st["ahead"] = gate_up(c + 1)
        act = (_silu(gate) * up).astype(BF16)
        part = _dot(act, wo_ref[c * FF_CHUNK:(c + 1) * FF_CHUNK, :])
        out["ffn"] = part if c == 0 else out["ffn"] + part

    return [(1, functools.partial(chunk, c)) for c in range(nchunks)]


def _swiglu_ffn(h, wi_ref, wo_ref):
    out = {}
    _run_interleaved(_swiglu_stages(h, wi_ref, wo_ref, out))
    return out["ffn"]


def _log2_forget(fpre, lb_ref):
    a0 = lb_ref[0:1, :]
    a1 = lb_ref[1:2, :]
    m = jnp.maximum(a0, a1)
    e0 = jnp.exp(a0 - m)
    e1 = jnp.exp(a1 - m)
    lb = e0 / (e0 + e1)
    return jnp.log(lb + (1.0 - lb) * jax.nn.sigmoid(fpre)) * (1.0 / math.log(2.0))


def _head_mean_square(x, seg_ref):
    seg = seg_ref[...]
    outs = []
    for j in range(x.shape[1] // V7X_LANES):
        xs = x[:, j * V7X_LANES:(j + 1) * V7X_LANES]
        sq = xs * xs
        hi = sq.astype(BF16)
        lo = (sq - hi.astype(F32)).astype(BF16)
        outs.append(_dot(jnp.concatenate([hi, lo], axis=1), seg))
    return outs[0] if len(outs) == 1 else jnp.concatenate(outs, axis=1)


def _rope(x, cos, sin_signed):
    lane = lax.broadcasted_iota(jnp.int32, cos.shape, 1)
    first_half = (lane & (AT_HEAD_DIM // 2)) == 0
    outs = []
    for j in range(x.shape[1] // V7X_LANES):
        xs = x[:, j * V7X_LANES:(j + 1) * V7X_LANES]
        partner = jnp.where(first_half,
                            pltpu.roll(xs, V7X_LANES - AT_HEAD_DIM // 2, axis=1),
                            pltpu.roll(xs, AT_HEAD_DIM // 2, axis=1))
        outs.append(xs * cos + partner * sin_signed)
    return outs[0] if len(outs) == 1 else jnp.concatenate(outs, axis=1)


def _qk_norm_rope(x, mean_square, gain, cos, sin_signed):
    return _rope(x * lax.rsqrt(mean_square + EPS) * gain, cos, sin_signed)


def _ffn_in_kernel(x_ref, n1_ref, wi_ref, wo_ref, n2_ref, win_ref, lbf_ref, lbb_ref, qn_ref, kn_ref,
                   cos_ref, sin_ref, seg_ref,
                   x1_ref, q_ref, lff_ref, lfb_ref, v_ref, g_ref, aq_ref, akv_ref):
    def half_stages(rows):
        st = {}

        def start():
            st["x"] = x_ref[rows, :]
            st["h"] = _rmsnorm(st["x"], n1_ref[...]).astype(BF16)

        def ffn_chunk(c):
            if c == 0:
                st["chunks"] = _swiglu_stages(st["h"], wi_ref, wo_ref, st)
            st["chunks"][c][1]()

        def residual():
            x1 = st["x"] + 0.5 * st["ffn"]
            x1_ref[rows, :] = x1
            st["h2"] = _rmsnorm(x1, n2_ref[...]).astype(BF16)

        def proj(lo, width):
            return _dot(st["h2"], win_ref[:, lo:lo + width])

        def attn_cols():
            st["aq"] = proj(COL_AQ, AT_WIDTH)
            st["akv"] = proj(COL_AKV, 2 * AT_KV_WIDTH)

        def forget_cols():
            lff_ref[rows, :] = _log2_forget(proj(COL_HF_FWD, HG_WIDTH), lbf_ref)
            lfb_ref[rows, :] = _log2_forget(proj(COL_HF_BWD, HG_WIDTH), lbb_ref)
            st["aq_ms"] = _head_mean_square(st["aq"], seg_ref)
            st["ak_ms"] = _head_mean_square(st["akv"][:, :AT_KV_WIDTH], seg_ref)

        def query_gate_cols():
            q_ref[rows, :] = _silu(proj(COL_HQ, HG_WIDTH))
            g_ref[rows, :] = _silu(proj(COL_HG, HG_WIDTH))

        def value_col():
            v_ref[rows, :] = proj(COL_HI, HG_WIDTH).astype(BF16)

        def rotary():
            cos = cos_ref[rows, :]
            sin = sin_ref[rows, :]
            aq = _qk_norm_rope(st["aq"], st["aq_ms"], qn_ref[...], cos, sin) * (1.0 / math.sqrt(AT_HEAD_DIM))
            aq_ref[rows, :] = aq.astype(BF16)
            akv = st["akv"]
            ak = _qk_norm_rope(akv[:, :AT_KV_WIDTH], st["ak_ms"], kn_ref[...], cos, sin)
            av = akv[:, AT_KV_WIDTH:]
            akv_ref[rows, :] = jnp.concatenate(
                [ak, pltpu.roll(ak, AT_HEAD_DIM, axis=1), av, pltpu.roll(av, AT_HEAD_DIM, axis=1)],
                axis=1).astype(BF16)

        return ([(0.1, start)] + [(3, functools.partial(ffn_chunk, c)) for c in range(D_FF // FF_CHUNK)]
                + [(0.1, residual), (3, attn_cols), (5, forget_cols), (0.1, rotary), (4, query_gate_cols),
                   (2, value_col)])

    tm = x_ref.shape[0]
    first = half_stages(slice(0, tm // 2))
    second = half_stages(slice(tm // 2, tm))
    shift = 0.5 * sum(weight for weight, _ in first)
    idle = lambda: None
    _run_interleaved(first + [(shift, idle)], [(shift, idle)] + second)


SUBLANES = 8
GROUP_LEVELS = 3


def _cumsum_rows(lf, tri_ref):
    hi = lf.astype(BF16)
    lo = (lf - hi.astype(F32)).astype(BF16)
    tri = tri_ref[...]
    return _dot(tri, hi) + _dot(tri, lo)


def _groups(a):
    return [a[g * SUBLANES:(g + 1) * SUBLANES, :] for g in range(a.shape[0] // SUBLANES)]


def _is_target_group(g, level, reverse):
    bit = (g >> (level - GROUP_LEVELS)) & 1
    return bit == (0 if reverse else 1)


def _gla_intra(q, lf, x, lvl, reverse):
    n = q.shape[0]
    ngroups = n // SUBLANES
    f = jnp.exp2(lf)
    k = 1.0 - f
    row = lax.broadcasted_iota(jnp.int32, q.shape, 0)
    lvl_g = _groups(lvl)
    a_g = [None] * ngroups

    def place(g, level, p_rows):
        keep = jnp.zeros_like(p_rows) if a_g[g] is None else a_g[g]
        a_g[g] = jnp.where(lvl_g[g] == level, p_rows, keep)

    for level in range(GROUP_LEVELS, SCAN_LEVELS):
        half = 1 << level
        q_rows, k_rows, targets = [], [], []
        for b in range(0, n, 2 * half):
            first, second = slice(b, b + half), slice(b + half, b + 2 * half)
            src, tgt = (second, first) if reverse else (first, second)
            r = b + half if reverse else b + half - 1
            x_r = x[r:r + 1, :]
            q_rows.append(q[tgt] * jnp.exp2(x[tgt] - x_r))
            k_src = k[src] * jnp.exp2(x_r - x[src])
            k_rows += [jnp.zeros_like(k_src), k_src] if reverse else [k_src, jnp.zeros_like(k_src)]
            targets += [g for g in range(b // SUBLANES, (b + 2 * half) // SUBLANES)
                        if _is_target_group(g, level, reverse)]
        p = _dot_nt(jnp.concatenate(q_rows, axis=0).astype(BF16), jnp.concatenate(k_rows, axis=0).astype(BF16))
        for i, g in enumerate(targets):
            place(g, level, p[i * SUBLANES:(i + 1) * SUBLANES, :])

    up = pltpu.roll(lf, n - 1, axis=0)
    down = pltpu.roll(lf, 1, axis=0)
    pos4 = row & 3
    if reverse:
        e1 = jnp.exp2(jnp.where(pos4 == 0, lf + up, jnp.where(pos4 == 1, lf, jnp.where(pos4 == 2, 0.0, down))))
        e0 = jnp.where((row & 1) == 0, f, 1.0)
        r2 = SUBLANES // 2
    else:
        e1 = jnp.exp2(jnp.where(pos4 == 0, up, jnp.where(pos4 == 1, 0.0, jnp.where(pos4 == 2, lf, lf + down))))
        e0 = jnp.where((row & 1) == 1, f, 1.0)
        r2 = SUBLANES // 2 - 1
    x_r2 = jnp.concatenate([jnp.broadcast_to(xg[r2:r2 + 1, :], xg.shape) for xg in _groups(x)], axis=0)
    e2 = jnp.exp2(-jnp.abs(x - x_r2))
    for level, e in enumerate((e0, e1, e2)):
        p = _dot_nt((q * e).astype(BF16), (k * e).astype(BF16))
        for g in range(ngroups):
            place(g, level, p[g * SUBLANES:(g + 1) * SUBLANES, :])

    a = jnp.concatenate(a_g, axis=0).astype(BF16)
    diag = jnp.sum(q * k, axis=-1, keepdims=True)
    x_end = x[0:1, :] if reverse else x[n - 1:n, :]
    q_in = (q * jnp.exp2(x)).astype(BF16)
    k_end = (k * jnp.exp2(x_end - x)).astype(BF16)
    return a, diag, q_in, k_end, jnp.exp2(x_end)


def _gla_inter(intra, v, state_ref):
    a, diag, q_in, k_end, decay = intra
    state = state_ref[...]
    o = _dot_nt(q_in, state.astype(BF16)) + _dot(a, v) + diag * v.astype(F32)
    state_ref[...] = state * decay + _dot_tn(v, k_end)
    return o


def _scan_kernel(lvl_f_ref, lvl_b_ref, tri_f_ref, tri_b_ref, qf_ref, lff_ref, vf_ref, qb_ref, lfb_ref, vb_ref,
                 of_ref, ob_ref, state_ref):
    @pl.when(pl.program_id(1) == 0)
    def _():
        state_ref[...] = jnp.zeros(state_ref.shape, F32)

    directions = ((qf_ref, lff_ref, vf_ref, of_ref, lvl_f_ref, tri_f_ref, False),
                  (qb_ref, lfb_ref, vb_ref, ob_ref, lvl_b_ref, tri_b_ref, True))
    heads = [(d, h) for d in range(2) for h in range(HG_HEADS)]
    cols = lambda h: slice(h * HG_DK, (h + 1) * HG_DK)
    nsub = qf_ref.shape[0] // SCAN_CHUNK
    chunk_rows = [(slice(j * SCAN_CHUNK, (j + 1) * SCAN_CHUNK),
                   slice((nsub - 1 - j) * SCAN_CHUNK, (nsub - j) * SCAN_CHUNK)) for j in range(nsub)]
    def intra(rows, cum, d, h):
        q_ref, lf_ref, _, _, lvl_ref, _, reverse = directions[d]
        return _gla_intra(q_ref[rows[d], cols(h)], lf_ref[rows[d], cols(h)], cum[d][:, cols(h)], lvl_ref[...],
                          reverse)

    def inter(rows, part, d, h):
        v_ref, o_ref = directions[d][2], directions[d][3]
        o_ref[rows[d], cols(h)] = _gla_inter(part, v_ref[rows[d], cols(h)], state_ref.at[d * HG_HEADS + h])

    pending = None
    for rows in chunk_rows:
        cum = [_cumsum_rows(lf_ref[rows[d], :], tri_ref)
               for d, (_, lf_ref, _, _, _, tri_ref, _) in enumerate(directions)]
        parts = []
        for i, (d, h) in enumerate(heads):
            parts.append(intra(rows, cum, d, h))
            if pending is not None:
                inter(pending[0], pending[1][i], d, h)
        pending = (rows, parts)
    for i, (d, h) in enumerate(heads):
        inter(pending[0], pending[1][i], d, h)


ATTN_LAG = 4


def _attn_stages(sink_ref, band_ref, q_ref, kvp_ref, kvc_ref, kvn_ref, o_ref, first_in_row, last_in_row):
    blk = ATT_BLOCK
    nsub = q_ref.shape[0] // blk
    st = {}

    def swapped(hd):
        return int(hd % 2 != hd // AT_GROUP)

    def prepare():
        neg_inf = jnp.float32(-jnp.inf)
        st["band_prev"] = band_ref[:, 0:blk]
        st["band_next"] = band_ref[:, blk:2 * blk]
        st["no_prev"] = jnp.where(first_in_row, neg_inf, 0.0)
        st["no_next"] = jnp.where(last_in_row, neg_inf, 0.0)
        lane = lax.broadcasted_iota(jnp.int32, (blk, V7X_LANES), 1)
        st["low"] = lane < AT_HEAD_DIM

        def kv_cat(col):
            sl = slice(col * V7X_LANES, (col + 1) * V7X_LANES)
            return jnp.concatenate([kvp_ref[:, sl], kvc_ref[:, sl], kvn_ref[:, sl]], axis=0)

        st["keys"] = (kv_cat(0), kv_cat(1))
        st["vals"] = (kv_cat(2), kv_cat(3))

    def scores(j):
        out = []
        for hd in range(AT_HEADS):
            q_pair = q_ref[j * blk:(j + 1) * blk, (hd // 2) * V7X_LANES:(hd // 2 + 1) * V7X_LANES]
            q = jnp.where(st["low"] if hd % 2 == 0 else ~st["low"], q_pair, jnp.zeros_like(q_pair))
            out.append(_dot_nt(q, st["keys"][swapped(hd)][j * blk:(j + 3) * blk]))
        st["scores", j] = out

    def softmax(j, pair):
        bias_prev = st["band_prev"] + st["no_prev"] if j == 0 else st["band_prev"]
        bias_next = st["band_next"] + st["no_next"] if j == nsub - 1 else st["band_next"]
        for side in range(2):
            hd = 2 * pair + side
            s = st["scores", j][hd]
            s_prev = s[:, 0:blk] + bias_prev
            s_cur = s[:, blk:2 * blk]
            s_next = s[:, 2 * blk:3 * blk] + bias_next
            sink = sink_ref[hd]
            m = jnp.max(jnp.maximum(jnp.maximum(s_prev, s_cur), s_next), axis=-1, keepdims=True)
            m = jnp.maximum(m, sink)
            p_prev = jnp.exp(s_prev - m)
            p_cur = jnp.exp(s_cur - m)
            p_next = jnp.exp(s_next - m)
            denom = jnp.sum(p_prev + p_cur + p_next, axis=-1, keepdims=True) + jnp.exp(sink - m)
            st["p", j, hd] = (jnp.concatenate([p_prev, p_cur, p_next], axis=1).astype(BF16), denom)

    def weighted_values(j, pair):
        res = []
        for side in range(2):
            hd = 2 * pair + side
            p, denom = st["p", j, hd]
            res.append(_dot(p, st["vals"][swapped(hd)][j * blk:(j + 3) * blk]) / denom)
        o_ref[j * blk:(j + 1) * blk, pair * V7X_LANES:(pair + 1) * V7X_LANES] = (
            jnp.where(st["low"], res[0], res[1]).astype(BF16))

    npairs = AT_HEADS // 2
    work = [(j, pair) for j in range(nsub) for pair in range(npairs)]
    stages = [(1, prepare)]
    for i in range(len(work) + ATTN_LAG):
        fns, weight = [], 0
        if i >= ATTN_LAG:
            fns.append(functools.partial(weighted_values, *work[i - ATTN_LAG]))
            weight += 1
        if i < len(work):
            if work[i][1] == 0:
                fns.append(functools.partial(scores, work[i][0]))
            fns.append(functools.partial(softmax, *work[i]))
            weight += 4
        stages.append((weight, lambda fns=fns: [fn() for fn in fns]))
    return stages


def _ffn_out_kernel(x1_ref, of_ref, ob_ref, g_ref, onorm_ref, wout_ref, n3_ref, wi_ref, wo_ref,
                    sink_ref, band_ref, q0_ref, kvp0_ref, kvc0_ref, kvn0_ref, q_ref, kvp_ref, kvc_ref, kvn_ref,
                    y_ref, oat_ref, *, tiles_per_seq):
    s = pl.program_id(0)
    nxt = jnp.minimum(s + 1, pl.num_programs(0) - 1)
    pos = lax.rem(nxt, tiles_per_seq)

    @pl.when(s == 0)
    def _():
        _run_interleaved(_attn_stages(sink_ref, band_ref, q0_ref, kvp0_ref, kvc0_ref, kvn0_ref, oat_ref.at[0],
                                      True, tiles_per_seq == 1))

    slot = lax.rem(s, 2)
    st = {}

    def mix():
        o = of_ref[...] + ob_ref[...]
        g = g_ref[...]
        onorm = onorm_ref[...]
        mixed = []
        for h in range(HG_HEADS):
            sl = slice(h * HG_DK, (h + 1) * HG_DK)
            mixed.append((_rmsnorm(o[:, sl], onorm) * g[:, sl]).astype(BF16))
        mixed.append(oat_ref[slot])
        st["x2"] = x1_ref[...] + _dot(jnp.concatenate(mixed, axis=1), wout_ref[...])
        st["h2"] = _rmsnorm(st["x2"], n3_ref[...]).astype(BF16)

    def ffn_chunk(c):
        if c == 0:
            st["chunks"] = _swiglu_stages(st["h2"], wi_ref, wo_ref, st)
        st["chunks"][c][1]()

    def finish():
        y_ref[...] = st["x2"] + 0.5 * st["ffn"]

    dense = ([(1, mix)] + [(3, functools.partial(ffn_chunk, c)) for c in range(D_FF // FF_CHUNK)] + [(1, finish)])
    _run_interleaved(dense, _attn_stages(sink_ref, band_ref, q_ref, kvp_ref, kvc_ref, kvn_ref, oat_ref.at[1 - slot],
                                         pos == 0, pos == tiles_per_seq - 1))


def _resident(shape):
    return pl.BlockSpec(shape, lambda *_: (0,) * len(shape), pipeline_mode=pl.Buffered(1))


def _params(semantics):
    return pltpu.CompilerParams(dimension_semantics=semantics, vmem_limit_bytes=V7X_VMEM_LIMIT)


def _scan_level_table(reverse):
    t = np.arange(SCAN_CHUNK)[:, None]
    s = np.arange(SCAN_CHUNK)[None, :]
    x = t ^ s
    lvl = np.where(x > 0, np.floor(np.log2(np.maximum(x, 1))).astype(np.int32), -1)
    valid = (s > t) if reverse else (s < t)
    return jnp.asarray(np.where(valid, lvl, -1).astype(np.int32))


def _band_bias():
    r = np.arange(ATT_BLOCK)[:, None]
    c = np.arange(ATT_BLOCK)[None, :]
    prev = np.where(c >= r, 0.0, -np.inf)
    nxt = np.where(c <= r, 0.0, -np.inf)
    return jnp.asarray(np.concatenate([prev, nxt], axis=1).astype(np.float32))


def _rope_tables(seq):
    pos = jnp.arange(seq, dtype=F32)
    inv_freq = ROPE_THETA ** (-jnp.arange(0, AT_HEAD_DIM, 2, dtype=F32) / AT_HEAD_DIM)
    ang = pos[:, None] * inv_freq[None, :]
    cos, sin = jnp.cos(ang), jnp.sin(ang)
    reps = V7X_LANES // AT_HEAD_DIM
    return jnp.tile(cos, (1, 2 * reps)), jnp.tile(jnp.concatenate([-sin, sin], axis=1), (1, reps))


def _layer(x, w):
    batch, seq, _ = x.shape
    tokens = batch * seq
    tm = TOKEN_TILE
    x2d = x.reshape(tokens, D_MODEL)
    tiles_per_seq = seq // tm

    row = lambda width: pl.BlockSpec((tm, width), lambda i: (i, 0))
    pos = lambda width: pl.BlockSpec((tm, width), lambda i: (i % tiles_per_seq, 0))
    sds = lambda width, dt: jax.ShapeDtypeStruct((tokens, width), dt)

    x1, q, lff, lfb, v, g, aq, akv = pl.pallas_call(
        _ffn_in_kernel,
        name="ffn_in",
        grid=(tokens // tm,),
        in_specs=[row(D_MODEL), _resident((1, D_MODEL)), _resident((D_MODEL, 2 * D_FF)),
                  _resident((D_FF, D_MODEL)), _resident((1, D_MODEL)), _resident((D_MODEL, N_IN)),
                  _resident((2, HG_WIDTH)), _resident((2, HG_WIDTH)), _resident((1, AT_WIDTH)),
                  _resident((1, AT_KV_WIDTH)), pos(V7X_LANES), pos(V7X_LANES),
                  _resident((2 * V7X_LANES, V7X_LANES))],
        out_specs=[row(D_MODEL), row(HG_WIDTH), row(HG_WIDTH), row(HG_WIDTH), row(HG_WIDTH), row(HG_WIDTH),
                   row(AT_WIDTH), row(4 * AT_KV_WIDTH)],
        out_shape=[sds(D_MODEL, F32), sds(HG_WIDTH, F32), sds(HG_WIDTH, F32), sds(HG_WIDTH, F32),
                   sds(HG_WIDTH, BF16), sds(HG_WIDTH, F32), sds(AT_WIDTH, BF16), sds(4 * AT_KV_WIDTH, BF16)],
        compiler_params=_params(("parallel",)),
    )(x2d, w["ffn1_norm"], w["ffn1_wi"], w["ffn1_wo"], w["mix_norm"], w["w_in"], w["hg_lb_fwd"],
      w["hg_lb_bwd"], w["q_norm"], w["k_norm"], w["cos"], w["sin"], w["seg"])

    c = SCAN_CHUNK
    nc = seq // SCAN_BLOCK
    fwd = lambda b, i: (b * nc + i, 0)
    bwd = lambda b, i: (b * nc + nc - 1 - i, 0)
    chunk = lambda imap: pl.BlockSpec((SCAN_BLOCK, HG_WIDTH), imap)
    o_f, o_b = pl.pallas_call(
        _scan_kernel,
        name="hgrn_scan",
        grid=(batch, nc),
        in_specs=[_resident((c, c)), _resident((c, c)), _resident((c, c)), _resident((c, c)),
                  chunk(fwd), chunk(fwd), chunk(fwd), chunk(bwd), chunk(bwd), chunk(bwd)],
        out_specs=[chunk(fwd), chunk(bwd)],
        out_shape=[sds(HG_WIDTH, F32), sds(HG_WIDTH, F32)],
        scratch_shapes=[pltpu.VMEM((2 * HG_HEADS, HG_DK, HG_DK), F32)],
        compiler_params=_params(("parallel", "arbitrary")),
    )(w["lvl_fwd"], w["lvl_bwd"], w["tri_fwd"], w["tri_bwd"], q, lff, v, q, lfb, v)

    blk = ATT_BLOCK
    per_tile = tm // blk
    ntiles = tokens // tm
    nblocks = tokens // blk
    ahead = lambda i: jnp.minimum(i + 1, ntiles - 1)
    kv_width = 4 * AT_KV_WIDTH
    before = lambda t: jnp.maximum(per_tile * t - 1, 0)
    after = lambda t: jnp.minimum(per_tile * t + per_tile, nblocks - 1)
    y = pl.pallas_call(
        functools.partial(_ffn_out_kernel, tiles_per_seq=tiles_per_seq),
        name="ffn_out",
        grid=(ntiles,),
        in_specs=[row(D_MODEL), row(HG_WIDTH), row(HG_WIDTH), row(HG_WIDTH),
                  _resident((1, HG_DK)), _resident((D_MODEL, D_MODEL)), _resident((1, D_MODEL)),
                  _resident((D_MODEL, 2 * D_FF)), _resident((D_FF, D_MODEL)),
                  pl.BlockSpec(memory_space=pltpu.SMEM), _resident((blk, 2 * blk)),
                  pl.BlockSpec((tm, AT_WIDTH), lambda i: (0, 0)),
                  pl.BlockSpec((blk, kv_width), lambda i: (0, 0)),
                  pl.BlockSpec((tm, kv_width), lambda i: (0, 0)),
                  pl.BlockSpec((blk, kv_width), lambda i: (min(per_tile, nblocks - 1), 0)),
                  pl.BlockSpec((tm, AT_WIDTH), lambda i: (ahead(i), 0)),
                  pl.BlockSpec((blk, kv_width), lambda i: (before(ahead(i)), 0)),
                  pl.BlockSpec((tm, kv_width), lambda i: (ahead(i), 0)),
                  pl.BlockSpec((blk, kv_width), lambda i: (after(ahead(i)), 0))],
        out_specs=row(D_MODEL),
        out_shape=sds(D_MODEL, F32),
        scratch_shapes=[pltpu.VMEM((2, tm, AT_WIDTH), BF16)],
        compiler_params=_params(("arbitrary",)),
    )(x1, o_f, o_b, g, w["hg_out_norm"], w["w_out"], w["ffn2_norm"], w["ffn2_wi"], w["ffn2_wo"],
      w["attn_sink"], w["band"], aq, akv, akv, akv, aq, akv, akv, akv)
    return y.reshape(batch, seq, D_MODEL)


def kernel(x_prompt, x_sample, ffn1_norm, ffn1_wi, ffn1_wo, mix_norm, w_in, hg_lb_fwd, hg_lb_bwd, hg_out_norm,
           q_norm, k_norm, attn_sink, w_out, ffn2_norm, ffn2_wi, ffn2_wo):
    seq = x_prompt.shape[1]
    assert x_sample.shape[1] == seq and seq % TOKEN_TILE == 0
    cos, sin = _rope_tables(seq)
    heads_per_group = V7X_LANES // AT_HEAD_DIM
    seg = jnp.kron(jnp.eye(heads_per_group, dtype=F32), jnp.full((AT_HEAD_DIM, AT_HEAD_DIM), 1.0 / AT_HEAD_DIM, F32))
    seg = jnp.concatenate([seg, seg], axis=0)
    w = {
        "ffn1_norm": ffn1_norm[0][None, :], "ffn1_wi": ffn1_wi[0].astype(BF16), "ffn1_wo": ffn1_wo[0].astype(BF16),
        "mix_norm": mix_norm[0][None, :], "w_in": w_in[0].astype(BF16),
        "hg_lb_fwd": hg_lb_fwd, "hg_lb_bwd": hg_lb_bwd, "hg_out_norm": hg_out_norm[0][None, :],
        "q_norm": jnp.tile(q_norm[0], AT_HEADS)[None, :], "k_norm": jnp.tile(k_norm[0], AT_KV_HEADS)[None, :],
        "attn_sink": attn_sink[0], "w_out": w_out[0].astype(BF16),
        "ffn2_norm": ffn2_norm[0][None, :], "ffn2_wi": ffn2_wi[0].astype(BF16), "ffn2_wo": ffn2_wo[0].astype(BF16),
        "cos": cos, "sin": sin, "seg": seg.astype(BF16),
        "lvl_fwd": _scan_level_table(False), "lvl_bwd": _scan_level_table(True), "band": _band_bias(),
        "tri_fwd": jnp.asarray(np.tril(np.ones((SCAN_CHUNK, SCAN_CHUNK), np.float32)), BF16),
        "tri_bwd": jnp.asarray(np.triu(np.ones((SCAN_CHUNK, SCAN_CHUNK), np.float32)), BF16),
    }
    return (_layer(x_prompt, w), _layer(x_sample, w))
```

```python
import functools
import math

import jax
import jax.numpy as jnp
import numpy as np
from jax import lax
from jax.experimental import pallas as pl
from jax.experimental.pallas import tpu as pltpu

D_MODEL = 1024
D_FF = 2816
HG_WIDTH = 512
HG_HEADS = 4
HG_DK = 128
AT_WIDTH = 512
AT_HEAD_DIM = 64
AT_HEADS = 8
AT_KV_HEADS = 2
AT_GROUP = AT_HEADS // AT_KV_HEADS
AT_KV_WIDTH = AT_KV_HEADS * AT_HEAD_DIM
WINDOW = 128
ROPE_THETA = 10000.0
EPS = 1e-6

COL_HQ = 0
COL_HF_FWD = COL_HQ + HG_WIDTH
COL_HF_BWD = COL_HF_FWD + HG_WIDTH
COL_HI = COL_HF_BWD + HG_WIDTH
COL_HG = COL_HI + HG_WIDTH
COL_AQ = COL_HG + HG_WIDTH
COL_AKV = COL_AQ + AT_WIDTH
N_IN = COL_AKV + 2 * AT_KV_WIDTH

V7X_LANES = 128
V7X_MXU_COLS = 256
V7X_VMEM_LIMIT = 58 * 1024 * 1024

TOKEN_TILE = 512
FF_CHUNK = V7X_MXU_COLS
SCAN_CHUNK = 128
SCAN_LEVELS = int(math.log2(SCAN_CHUNK))
SCAN_BLOCK = 8 * SCAN_CHUNK
ATT_BLOCK = WINDOW

BF16 = jnp.bfloat16
F32 = jnp.float32


def _dot(a, b):
    return jnp.dot(a, b, preferred_element_type=F32)


def _dot_nt(a, b):
    return lax.dot_general(a, b, (((1,), (1,)), ((), ())), preferred_element_type=F32)


def _dot_tn(a, b):
    return lax.dot_general(a, b, (((0,), (0,)), ((), ())), preferred_element_type=F32)


def _rmsnorm(x, gain):
    ms = jnp.mean(x * x, axis=-1, keepdims=True)
    return x * lax.rsqrt(ms + EPS) * gain


def _silu(x):
    return x * jax.nn.sigmoid(x)


def _run_interleaved(*stage_lists):
    order = []
    for k, stages in enumerate(stage_lists):
        total = float(sum(weight for weight, _ in stages))
        done = 0.0
        for i, (weight, _) in enumerate(stages):
            order.append(((done + 0.5 * weight) / total, k, i))
            done += weight
    for _, k, i in sorted(order):
        stage_lists[k][i][1]()


def _swiglu_stages(h, wi_ref, wo_ref, out):
    nchunks = D_FF // FF_CHUNK
    st = {}

    def gate_up(c):
        lo = c * FF_CHUNK
        return _dot(h, wi_ref[:, lo:lo + FF_CHUNK]), _dot(h, wi_ref[:, D_FF + lo:D_FF + lo + FF_CHUNK])

    def chunk(c):
        if c == 0:
            st["ahead"] = gate_up(0)
        gate, up = st["ahead"]
        if c + 1 < nchunks:
            st["ahead"] = gate_up(c + 1)
        act = (_silu(gate) * up).astype(BF16)
        part = _dot(act, wo_ref[c * FF_CHUNK:(c + 1) * FF_CHUNK, :])
        out["ffn"] = part if c == 0 else out["ffn"] + part

    return [(1, functools.partial(chunk, c)) for c in range(nchunks)]


def _swiglu_ffn(h, wi_ref, wo_ref):
    out = {}
    _run_interleaved(_swiglu_stages(h, wi_ref, wo_ref, out))
    return out["ffn"]


def _log2_forget(fpre, lb_ref):
    a0 = lb_ref[0:1, :]
    a1 = lb_ref[1:2, :]
    m = jnp.maximum(a0, a1)
    e0 = jnp.exp(a0 - m)
    e1 = jnp.exp(a1 - m)
    lb = e0 / (e0 + e1)
    return jnp.log(lb + (1.0 - lb) * jax.nn.sigmoid(fpre)) * (1.0 / math.log(2.0))


def _head_mean_square(x, seg_ref):
    seg = seg_ref[...]
    outs = []
    for j in range(x.shape[1] // V7X_LANES):
        xs = x[:, j * V7X_LANES:(j + 1) * V7X_LANES]
        sq = xs * xs
        hi = sq.astype(BF16)
        lo = (sq - hi.astype(F32)).astype(BF16)
        outs.append(_dot(jnp.concatenate([hi, lo], axis=1), seg))
    return outs[0] if len(outs) == 1 else jnp.concatenate(outs, axis=1)


def _rope(x, cos, sin_signed):
    lane = lax.broadcasted_iota(jnp.int32, cos.shape, 1)
    first_half = (lane & (AT_HEAD_DIM // 2)) == 0
    outs = []
    for j in range(x.shape[1] // V7X_LANES):
        xs = x[:, j * V7X_LANES:(j + 1) * V7X_LANES]
        partner = jnp.where(first_half,
                            pltpu.roll(xs, V7X_LANES - AT_HEAD_DIM // 2, axis=1),
                            pltpu.roll(xs, AT_HEAD_DIM // 2, axis=1))
        outs.append(xs * cos + partner * sin_signed)
    return outs[0] if len(outs) == 1 else jnp.concatenate(outs, axis=1)


def _qk_norm_rope(x, mean_square, gain, cos, sin_signed):
    return _rope(x * lax.rsqrt(mean_square + EPS) * gain, cos, sin_signed)


def _ffn_in_kernel(x_ref, n1_ref, wi_ref, wo_ref, n2_ref, win_ref, lbf_ref, lbb_ref, qn_ref, kn_ref,
                   cos_ref, sin_ref, seg_ref,
                   x1_ref, q_ref, lff_ref, lfb_ref, v_ref, g_ref, aq_ref, akv_ref):
    def half_stages(rows):
        st = {}

        def start():
            st["x"] = x_ref[rows, :]
            st["h"] = _rmsnorm(st["x"], n1_ref[...]).astype(BF16)

        def ffn_chunk(c):
            if c == 0:
                st["chunks"] = _swiglu_stages(st["h"], wi_ref, wo_ref, st)
            st["chunks"][c][1]()

        def residual():
            x1 = st["x"] + 0.5 * st["ffn"]
            x1_ref[rows, :] = x1
            st["h2"] = _rmsnorm(x1, n2_ref[...]).astype(BF16)

        def proj(lo, width):
            return _dot(st["h2"], win_ref[:, lo:lo + width])

        def attn_cols():
            st["aq"] = proj(COL_AQ, AT_WIDTH)
            st["akv"] = proj(COL_AKV, 2 * AT_KV_WIDTH)

        def forget_cols():
            lff_ref[rows, :] = _log2_forget(proj(COL_HF_FWD, HG_WIDTH), lbf_ref)
            lfb_ref[rows, :] = _log2_forget(proj(COL_HF_BWD, HG_WIDTH), lbb_ref)
            st["aq_ms"] = _head_mean_square(st["aq"], seg_ref)
            st["ak_ms"] = _head_mean_square(st["akv"][:, :AT_KV_WIDTH], seg_ref)

        def query_gate_cols():
            q_ref[rows, :] = _silu(proj(COL_HQ, HG_WIDTH))
            g_ref[rows, :] = _silu(proj(COL_HG, HG_WIDTH))

        def value_col():
            v_ref[rows, :] = proj(COL_HI, HG_WIDTH).astype(BF16)

        def rotary():
            cos = cos_ref[rows, :]
            sin = sin_ref[rows, :]
            aq = _qk_norm_rope(st["aq"], st["aq_ms"], qn_ref[...], cos, sin) * (1.0 / math.sqrt(AT_HEAD_DIM))
            aq_ref[rows, :] = aq.astype(BF16)
            akv = st["akv"]
            ak = _qk_norm_rope(akv[:, :AT_KV_WIDTH], st["ak_ms"], kn_ref[...], cos, sin)
            av = akv[:, AT_KV_WIDTH:]
            akv_ref[rows, :] = jnp.concatenate(
                [ak, pltpu.roll(ak, AT_HEAD_DIM, axis=1), av, pltpu.roll(av, AT_HEAD_DIM, axis=1)],
                axis=1).astype(BF16)

        return ([(0.1, start)] + [(3, functools.partial(ffn_chunk, c)) for c in range(D_FF // FF_CHUNK)]
                + [(0.1, residual), (3, attn_cols), (5, forget_cols), (0.1, rotary), (4, query_gate_cols),
                   (2, value_col)])

    tm = x_ref.shape[0]
    first = half_stages(slice(0, tm // 2))
    second = half_stages(slice(tm // 2, tm))
    shift = 0.5 * sum(weight for weight, _ in first)
    idle = lambda: None
    _run_interleaved(first + [(shift, idle)], [(shift, idle)] + second)


SUBLANES = 8
GROUP_LEVELS = 3


def _cumsum_rows(lf, tri_ref):
    hi = lf.astype(BF16)
    lo = (lf - hi.astype(F32)).astype(BF16)
    tri = tri_ref[...]
    return _dot(tri, hi) + _dot(tri, lo)


def _groups(a):
    return [a[g * SUBLANES:(g + 1) * SUBLANES, :] for g in range(a.shape[0] // SUBLANES)]


def _is_target_group(g, level, reverse):
    bit = (g >> (level - GROUP_LEVELS)) & 1
    return bit == (0 if reverse else 1)


def _gla_intra(q, lf, x, lvl, reverse):
    n = q.shape[0]
    ngroups = n // SUBLANES
    f = jnp.exp2(lf)
    k = 1.0 - f
    row = lax.broadcasted_iota(jnp.int32, q.shape, 0)
    lvl_g = _groups(lvl)
    a_g = [None] * ngroups

    def place(g, level, p_rows):
        keep = jnp.zeros_like(p_rows) if a_g[g] is None else a_g[g]
        a_g[g] = jnp.where(lvl_g[g] == level, p_rows, keep)

    for level in range(GROUP_LEVELS, SCAN_LEVELS):
        half = 1 << level
        q_rows, k_rows, targets = [], [], []
        for b in range(0, n, 2 * half):
            first, second = slice(b, b + half), slice(b + half, b + 2 * half)
            src, tgt = (second, first) if reverse else (first, second)
            r = b + half if reverse else b + half - 1
            x_r = x[r:r + 1, :]
            q_rows.append(q[tgt] * jnp.exp2(x[tgt] - x_r))
            k_src = k[src] * jnp.exp2(x_r - x[src])
            k_rows += [jnp.zeros_like(k_src), k_src] if reverse else [k_src, jnp.zeros_like(k_src)]
            targets += [g for g in range(b // SUBLANES, (b + 2 * half) // SUBLANES)
                        if _is_target_group(g, level, reverse)]
        p = _dot_nt(jnp.concatenate(q_rows, axis=0).astype(BF16), jnp.concatenate(k_rows, axis=0).astype(BF16))
        for i, g in enumerate(targets):
            place(g, level, p[i * SUBLANES:(i + 1) * SUBLANES, :])

    up = pltpu.roll(lf, n - 1, axis=0)
    down = pltpu.roll(lf, 1, axis=0)
    pos4 = row & 3
    if reverse:
        e1 = jnp.exp2(jnp.where(pos4 == 0, lf + up, jnp.where(pos4 == 1, lf, jnp.where(pos4 == 2, 0.0, down))))
        e0 = jnp.where((row & 1) == 0, f, 1.0)
        r2 = SUBLANES // 2
    else:
        e1 = jnp.exp2(jnp.where(pos4 == 0, up, jnp.where(pos4 == 1, 0.0, jnp.where(pos4 == 2, lf, lf + down))))
        e0 = jnp.where((row & 1) == 1, f, 1.0)
        r2 = SUBLANES // 2 - 1
    x_r2 = jnp.concatenate([jnp.broadcast_to(xg[r2:r2 + 1, :], xg.shape) for xg in _groups(x)], axis=0)
    e2 = jnp.exp2(-jnp.abs(x - x_r2))
    for level, e in enumerate((e0, e1, e2)):
        p = _dot_nt((q * e).astype(BF16), (k * e).astype(BF16))
        for g in range(ngroups):
            place(g, level, p[g * SUBLANES:(g + 1) * SUBLANES, :])

    a = jnp.concatenate(a_g, axis=0).astype(BF16)
    diag = jnp.sum(q * k, axis=-1, keepdims=True)
    x_end = x[0:1, :] if reverse else x[n - 1:n, :]
    q_in = (q * jnp.exp2(x)).astype(BF16)
    k_end = (k * jnp.exp2(x_end - x)).astype(BF16)
    return a, diag, q_in, k_end, jnp.exp2(x_end)


def _gla_inter(intra, v, state_ref):
    a, diag, q_in, k_end, decay = intra
    state = state_ref[...]
    o = _dot_nt(q_in, state.astype(BF16)) + _dot(a, v) + diag * v.astype(F32)
    state_ref[...] = state * decay + _dot_tn(v, k_end)
    return o


def _scan_kernel(lvl_f_ref, lvl_b_ref, tri_f_ref, tri_b_ref, qf_ref, lff_ref, vf_ref, qb_ref, lfb_ref, vb_ref,
                 of_ref, ob_ref, state_ref):
    @pl.when(pl.program_id(1) == 0)
    def _():
        state_ref[...] = jnp.zeros(state_ref.shape, F32)

    directions = ((qf_ref, lff_ref, vf_ref, of_ref, lvl_f_ref, tri_f_ref, False),
                  (qb_ref, lfb_ref, vb_ref, ob_ref, lvl_b_ref, tri_b_ref, True))
    heads = [(d, h) for d in range(2) for h in range(HG_HEADS)]
    cols = lambda h: slice(h * HG_DK, (h + 1) * HG_DK)
    nsub = qf_ref.shape[0] // SCAN_CHUNK
    chunk_rows = [(slice(j * SCAN_CHUNK, (j + 1) * SCAN_CHUNK),
                   slice((nsub - 1 - j) * SCAN_CHUNK, (nsub - j) * SCAN_CHUNK)) for j in range(nsub)]
    def intra(rows, cum, d, h):
        q_ref, lf_ref, _, _, lvl_ref, _, reverse = directions[d]
        return _gla_intra(q_ref[rows[d], cols(h)], lf_ref[rows[d], cols(h)], cum[d][:, cols(h)], lvl_ref[...],
                          reverse)

    def inter(rows, part, d, h):
        v_ref, o_ref = directions[d][2], directions[d][3]
        o_ref[rows[d], cols(h)] = _gla_inter(part, v_ref[rows[d], cols(h)], state_ref.at[d * HG_HEADS + h])

    pending = None
    for rows in chunk_rows:
        cum = [_cumsum_rows(lf_ref[rows[d], :], tri_ref)
               for d, (_, lf_ref, _, _, _, tri_ref, _) in enumerate(directions)]
        parts = []
        for i, (d, h) in enumerate(heads):
            parts.append(intra(rows, cum, d, h))
            if pending is not None:
                inter(pending[0], pending[1][i], d, h)
        pending = (rows, parts)
    for i, (d, h) in enumerate(heads):
        inter(pending[0], pending[1][i], d, h)


ATTN_LAG = 4


def _attn_stages(sink_ref, band_ref, q_ref, kvp_ref, kvc_ref, kvn_ref, o_ref, first_in_row, last_in_row):
    blk = ATT_BLOCK
    nsub = q_ref.shape[0] // blk
    st = {}

    def swapped(hd):
        return int(hd % 2 != hd // AT_GROUP)

    def prepare():
        neg_inf = jnp.float32(-jnp.inf)
        st["band_prev"] = band_ref[:, 0:blk]
        st["band_next"] = band_ref[:, blk:2 * blk]
        st["no_prev"] = jnp.where(first_in_row, neg_inf, 0.0)
        st["no_next"] = jnp.where(last_in_row, neg_inf, 0.0)
        lane = lax.broadcasted_iota(jnp.int32, (blk, V7X_LANES), 1)
        st["low"] = lane < AT_HEAD_DIM

        def kv_cat(col):
            sl = slice(col * V7X_LANES, (col + 1) * V7X_LANES)
            return jnp.concatenate([kvp_ref[:, sl], kvc_ref[:, sl], kvn_ref[:, sl]], axis=0)

        st["keys"] = (kv_cat(0), kv_cat(1))
        st["vals"] = (kv_cat(2), kv_cat(3))

    stacks = [[hd for hd in range(AT_HEADS) if swapped(hd) == which] for which in range(2)]

    def scores(j):
        out = [None] * AT_HEADS
        for which, heads in enumerate(stacks):
            qs = []
            for hd in heads:
                q_pair = q_ref[j * blk:(j + 1) * blk, (hd // 2) * V7X_LANES:(hd // 2 + 1) * V7X_LANES]
                qs.append(jnp.where(st["low"] if hd % 2 == 0 else ~st["low"], q_pair, jnp.zeros_like(q_pair)))
            s = _dot_nt(jnp.concatenate(qs, axis=0), st["keys"][which][j * blk:(j + 3) * blk])
            for n, hd in enumerate(heads):
                out[hd] = s[n * blk:(n + 1) * blk, :]
        st["scores", j] = out

    def softmax(j, pair):
        bias_prev = st["band_prev"] + st["no_prev"] if j == 0 else st["band_prev"]
        bias_next = st["band_next"] + st["no_next"] if j == nsub - 1 else st["band_next"]
        for side in range(2):
            hd = 2 * pair + side
            s = st["scores", j][hd]
            s_prev = s[:, 0:blk] + bias_prev
            s_cur = s[:, blk:2 * blk]
            s_next = s[:, 2 * blk:3 * blk] + bias_next
            sink = sink_ref[hd]
            m = jnp.max(jnp.maximum(jnp.maximum(s_prev, s_cur), s_next), axis=-1, keepdims=True)
            m = jnp.maximum(m, sink)
            p_prev = jnp.exp(s_prev - m)
            p_cur = jnp.exp(s_cur - m)
            p_next = jnp.exp(s_next - m)
            denom = jnp.sum(p_prev + p_cur + p_next, axis=-1, keepdims=True) + jnp.exp(sink - m)
            st["p", j, hd] = (jnp.concatenate([p_prev, p_cur, p_next], axis=1).astype(BF16), denom)

    def weighted_values(j):
        res = [None] * AT_HEADS
        for which, heads in enumerate(stacks):
            p = jnp.concatenate([st["p", j, hd][0] for hd in heads], axis=0)
            r = _dot(p, st["vals"][which][j * blk:(j + 3) * blk])
            for n, hd in enumerate(heads):
                res[hd] = r[n * blk:(n + 1) * blk, :] / st["p", j, hd][1]
        for pair in range(AT_HEADS // 2):
            o_ref[j * blk:(j + 1) * blk, pair * V7X_LANES:(pair + 1) * V7X_LANES] = (
                jnp.where(st["low"], res[2 * pair], res[2 * pair + 1]).astype(BF16))

    npairs = AT_HEADS // 2
    work = [(j, pair) for j in range(nsub) for pair in range(npairs)]
    stages = [(1, prepare)]
    for i in range(len(work) + ATTN_LAG):
        fns, weight = [], 0
        done = i - ATTN_LAG
        if done >= 0 and work[done][1] == npairs - 1:
            fns.append(functools.partial(weighted_values, work[done][0]))
            weight += 4
        if i < len(work):
            if work[i][1] == 0:
                fns.append(functools.partial(scores, work[i][0]))
            fns.append(functools.partial(softmax, *work[i]))
            weight += 4
        if fns:
            stages.append((weight, lambda fns=fns: [fn() for fn in fns]))
    return stages


def _ffn_out_kernel(x1_ref, of_ref, ob_ref, g_ref, onorm_ref, wout_ref, n3_ref, wi_ref, wo_ref,
                    sink_ref, band_ref, q0_ref, kvp0_ref, kvc0_ref, kvn0_ref, q_ref, kvp_ref, kvc_ref, kvn_ref,
                    y_ref, oat_ref, *, tiles_per_seq):
    s = pl.program_id(0)
    nxt = jnp.minimum(s + 1, pl.num_programs(0) - 1)
    pos = lax.rem(nxt, tiles_per_seq)

    @pl.when(s == 0)
    def _():
        _run_interleaved(_attn_stages(sink_ref, band_ref, q0_ref, kvp0_ref, kvc0_ref, kvn0_ref, oat_ref.at[0],
                                      True, tiles_per_seq == 1))

    slot = lax.rem(s, 2)
    st = {}

    def mix():
        o = of_ref[...] + ob_ref[...]
        g = g_ref[...]
        onorm = onorm_ref[...]
        mixed = []
        for h in range(HG_HEADS):
            sl = slice(h * HG_DK, (h + 1) * HG_DK)
            mixed.append((_rmsnorm(o[:, sl], onorm) * g[:, sl]).astype(BF16))
        mixed.append(oat_ref[slot])
        st["x2"] = x1_ref[...] + _dot(jnp.concatenate(mixed, axis=1), wout_ref[...])
        st["h2"] = _rmsnorm(st["x2"], n3_ref[...]).astype(BF16)

    def ffn_chunk(c):
        if c == 0:
            st["chunks"] = _swiglu_stages(st["h2"], wi_ref, wo_ref, st)
        st["chunks"][c][1]()

    def finish():
        y_ref[...] = st["x2"] + 0.5 * st["ffn"]

    dense = ([(1, mix)] + [(3, functools.partial(ffn_chunk, c)) for c in range(D_FF // FF_CHUNK)] + [(1, finish)])
    _run_interleaved(dense, _attn_stages(sink_ref, band_ref, q_ref, kvp_ref, kvc_ref, kvn_ref, oat_ref.at[1 - slot],
                                         pos == 0, pos == tiles_per_seq - 1))


def _resident(shape):
    return pl.BlockSpec(shape, lambda *_: (0,) * len(shape), pipeline_mode=pl.Buffered(1))


def _params(semantics):
    return pltpu.CompilerParams(dimension_semantics=semantics, vmem_limit_bytes=V7X_VMEM_LIMIT)


def _scan_level_table(reverse):
    t = np.arange(SCAN_CHUNK)[:, None]
    s = np.arange(SCAN_CHUNK)[None, :]
    x = t ^ s
    lvl = np.where(x > 0, np.floor(np.log2(np.maximum(x, 1))).astype(np.int32), -1)
    valid = (s > t) if reverse else (s < t)
    return jnp.asarray(np.where(valid, lvl, -1).astype(np.int32))


def _band_bias():
    r = np.arange(ATT_BLOCK)[:, None]
    c = np.arange(ATT_BLOCK)[None, :]
    prev = np.where(c >= r, 0.0, -np.inf)
    nxt = np.where(c <= r, 0.0, -np.inf)
    return jnp.asarray(np.concatenate([prev, nxt], axis=1).astype(np.float32))


def _rope_tables(seq):
    pos = jnp.arange(seq, dtype=F32)
    inv_freq = ROPE_THETA ** (-jnp.arange(0, AT_HEAD_DIM, 2, dtype=F32) / AT_HEAD_DIM)
    ang = pos[:, None] * inv_freq[None, :]
    cos, sin = jnp.cos(ang), jnp.sin(ang)
    reps = V7X_LANES // AT_HEAD_DIM
    return jnp.tile(cos, (1, 2 * reps)), jnp.tile(jnp.concatenate([-sin, sin], axis=1), (1, reps))


def _layer(x, w):
    batch, seq, _ = x.shape
    tokens = batch * seq
    tm = TOKEN_TILE
    x2d = x.reshape(tokens, D_MODEL)
    tiles_per_seq = seq // tm

    row = lambda width: pl.BlockSpec((tm, width), lambda i: (i, 0))
    pos = lambda width: pl.BlockSpec((tm, width), lambda i: (i % tiles_per_seq, 0))
    sds = lambda width, dt: jax.ShapeDtypeStruct((tokens, width), dt)

    x1, q, lff, lfb, v, g, aq, akv = pl.pallas_call(
        _ffn_in_kernel,
        name="ffn_in",
        grid=(tokens // tm,),
        in_specs=[row(D_MODEL), _resident((1, D_MODEL)), _resident((D_MODEL, 2 * D_FF)),
                  _resident((D_FF, D_MODEL)), _resident((1, D_MODEL)), _resident((D_MODEL, N_IN)),
                  _resident((2, HG_WIDTH)), _resident((2, HG_WIDTH)), _resident((1, AT_WIDTH)),
                  _resident((1, AT_KV_WIDTH)), pos(V7X_LANES), pos(V7X_LANES),
                  _resident((2 * V7X_LANES, V7X_LANES))],
        out_specs=[row(D_MODEL), row(HG_WIDTH), row(HG_WIDTH), row(HG_WIDTH), row(HG_WIDTH), row(HG_WIDTH),
                   row(AT_WIDTH), row(4 * AT_KV_WIDTH)],
        out_shape=[sds(D_MODEL, F32), sds(HG_WIDTH, F32), sds(HG_WIDTH, F32), sds(HG_WIDTH, F32),
                   sds(HG_WIDTH, BF16), sds(HG_WIDTH, F32), sds(AT_WIDTH, BF16), sds(4 * AT_KV_WIDTH, BF16)],
        compiler_params=_params(("parallel",)),
    )(x2d, w["ffn1_norm"], w["ffn1_wi"], w["ffn1_wo"], w["mix_norm"], w["w_in"], w["hg_lb_fwd"],
      w["hg_lb_bwd"], w["q_norm"], w["k_norm"], w["cos"], w["sin"], w["seg"])

    c = SCAN_CHUNK
    nc = seq // SCAN_BLOCK
    fwd = lambda b, i: (b * nc + i, 0)
    bwd = lambda b, i: (b * nc + nc - 1 - i, 0)
    chunk = lambda imap: pl.BlockSpec((SCAN_BLOCK, HG_WIDTH), imap)
    o_f, o_b = pl.pallas_call(
        _scan_kernel,
        name="hgrn_scan",
        grid=(batch, nc),
        in_specs=[_resident((c, c)), _resident((c, c)), _resident((c, c)), _resident((c, c)),
                  chunk(fwd), chunk(fwd), chunk(fwd), chunk(bwd), chunk(bwd), chunk(bwd)],
        out_specs=[chunk(fwd), chunk(bwd)],
        out_shape=[sds(HG_WIDTH, F32), sds(HG_WIDTH, F32)],
        scratch_shapes=[pltpu.VMEM((2 * HG_HEADS, HG_DK, HG_DK), F32)],
        compiler_params=_params(("parallel", "arbitrary")),
    )(w["lvl_fwd"], w["lvl_bwd"], w["tri_fwd"], w["tri_bwd"], q, lff, v, q, lfb, v)

    blk = ATT_BLOCK
    per_tile = tm // blk
    ntiles = tokens // tm
    nblocks = tokens // blk
    ahead = lambda i: jnp.minimum(i + 1, ntiles - 1)
    kv_width = 4 * AT_KV_WIDTH
    before = lambda t: jnp.maximum(per_tile * t - 1, 0)
    after = lambda t: jnp.minimum(per_tile * t + per_tile, nblocks - 1)
    y = pl.pallas_call(
        functools.partial(_ffn_out_kernel, tiles_per_seq=tiles_per_seq),
        name="ffn_out",
        grid=(ntiles,),
        in_specs=[row(D_MODEL), row(HG_WIDTH), row(HG_WIDTH), row(HG_WIDTH),
                  _resident((1, HG_DK)), _resident((D_MODEL, D_MODEL)), _resident((1, D_MODEL)),
                  _resident((D_MODEL, 2 * D_FF)), _resident((D_FF, D_MODEL)),
                  pl.BlockSpec(memory_space=pltpu.SMEM), _resident((blk, 2 * blk)),
                  pl.BlockSpec((tm, AT_WIDTH), lambda i: (0, 0)),
                  pl.BlockSpec((blk, kv_width), lambda i: (0, 0)),
                  pl.BlockSpec((tm, kv_width), lambda i: (0, 0)),
                  pl.BlockSpec((blk, kv_width), lambda i: (min(per_tile, nblocks - 1), 0)),
                  pl.BlockSpec((tm, AT_WIDTH), lambda i: (ahead(i), 0)),
                  pl.BlockSpec((blk, kv_width), lambda i: (before(ahead(i)), 0)),
                  pl.BlockSpec((tm, kv_width), lambda i: (ahead(i), 0)),
                  pl.BlockSpec((blk, kv_width), lambda i: (after(ahead(i)), 0))],
        out_specs=row(D_MODEL),
        out_shape=sds(D_MODEL, F32),
        scratch_shapes=[pltpu.VMEM((2, tm, AT_WIDTH), BF16)],
        compiler_params=_params(("arbitrary",)),
    )(x1, o_f, o_b, g, w["hg_out_norm"], w["w_out"], w["ffn2_norm"], w["ffn2_wi"], w["ffn2_wo"],
      w["attn_sink"], w["band"], aq, akv, akv, akv, aq, akv, akv, akv)
    return y.reshape(batch, seq, D_MODEL)


def kernel(x_prompt, x_sample, ffn1_norm, ffn1_wi, ffn1_wo, mix_norm, w_in, hg_lb_fwd, hg_lb_bwd, hg_out_norm,
           q_norm, k_norm, attn_sink, w_out, ffn2_norm, ffn2_wi, ffn2_wo):
    seq = x_prompt.shape[1]
    assert x_sample.shape[1] == seq and seq % TOKEN_TILE == 0
    cos, sin = _rope_tables(seq)
    heads_per_group = V7X_LANES // AT_HEAD_DIM
    seg = jnp.kron(jnp.eye(heads_per_group, dtype=F32), jnp.full((AT_HEAD_DIM, AT_HEAD_DIM), 1.0 / AT_HEAD_DIM, F32))
    seg = jnp.concatenate([seg, seg], axis=0)
    w = {
        "ffn1_norm": ffn1_norm[0][None, :], "ffn1_wi": ffn1_wi[0].astype(BF16), "ffn1_wo": ffn1_wo[0].astype(BF16),
        "mix_norm": mix_norm[0][None, :], "w_in": w_in[0].astype(BF16),
        "hg_lb_fwd": hg_lb_fwd, "hg_lb_bwd": hg_lb_bwd, "hg_out_norm": hg_out_norm[0][None, :],
        "q_norm": jnp.tile(q_norm[0], AT_HEADS)[None, :], "k_norm": jnp.tile(k_norm[0], AT_KV_HEADS)[None, :],
        "attn_sink": attn_sink[0], "w_out": w_out[0].astype(BF16),
        "ffn2_norm": ffn2_norm[0][None, :], "ffn2_wi": ffn2_wi[0].astype(BF16), "ffn2_wo": ffn2_wo[0].astype(BF16),
        "cos": cos, "sin": sin, "seg": seg.astype(BF16),
        "lvl_fwd": _scan_level_table(False), "lvl_bwd": _scan_level_table(True), "band": _band_bias(),
        "tri_fwd": jnp.asarray(np.tril(np.ones((SCAN_CHUNK, SCAN_CHUNK), np.float32)), BF16),
        "tri_bwd": jnp.asarray(np.triu(np.ones((SCAN_CHUNK, SCAN_CHUNK), np.float32)), BF16),
    }
    return (_layer(x_prompt, w), _layer(x_sample, w))
```

```python
import functools
import math

import jax
import jax.numpy as jnp
import numpy as np
from jax import lax
from jax.experimental import pallas as pl
from jax.experimental.pallas import tpu as pltpu

D_MODEL = 1024
D_FF = 2816
HG_WIDTH = 512
HG_HEADS = 4
HG_DK = 128
AT_WIDTH = 512
AT_HEAD_DIM = 64
AT_HEADS = 8
AT_KV_HEADS = 2
AT_GROUP = AT_HEADS // AT_KV_HEADS
AT_KV_WIDTH = AT_KV_HEADS * AT_HEAD_DIM
WINDOW = 128
ROPE_THETA = 10000.0
EPS = 1e-6

COL_HQ = 0
COL_HF_FWD = COL_HQ + HG_WIDTH
COL_HF_BWD = COL_HF_FWD + HG_WIDTH
COL_HI = COL_HF_BWD + HG_WIDTH
COL_HG = COL_HI + HG_WIDTH
COL_AQ = COL_HG + HG_WIDTH
COL_AKV = COL_AQ + AT_WIDTH
N_IN = COL_AKV + 2 * AT_KV_WIDTH

V7X_LANES = 128
V7X_MXU_COLS = 256
V7X_VMEM_LIMIT = 58 * 1024 * 1024

TOKEN_TILE = 512
FF_CHUNK = V7X_MXU_COLS
FFN_IN_SLICES = 4
SCAN_CHUNK = 128
SCAN_LEVELS = int(math.log2(SCAN_CHUNK))
SCAN_BLOCK = 8 * SCAN_CHUNK
ATT_BLOCK = WINDOW

BF16 = jnp.bfloat16
F32 = jnp.float32


def _dot(a, b):
    return jnp.dot(a, b, preferred_element_type=F32)


def _dot_nt(a, b):
    return lax.dot_general(a, b, (((1,), (1,)), ((), ())), preferred_element_type=F32)


def _dot_tn(a, b):
    return lax.dot_general(a, b, (((0,), (0,)), ((), ())), preferred_element_type=F32)


def _rmsnorm(x, gain):
    ms = jnp.mean(x * x, axis=-1, keepdims=True)
    return x * lax.rsqrt(ms + EPS) * gain


def _silu(x):
    return x * jax.nn.sigmoid(x)


def _run_interleaved(*stage_lists):
    order = []
    for k, stages in enumerate(stage_lists):
        total = float(sum(weight for weight, _ in stages))
        done = 0.0
        for i, (weight, _) in enumerate(stages):
            order.append(((done + 0.5 * weight) / total, k, i))
            done += weight
    for _, k, i in sorted(order):
        stage_lists[k][i][1]()


def _swiglu_stages(h, wi_ref, wo_ref, out):
    nchunks = D_FF // FF_CHUNK
    st = {}

    def gate_up(c):
        lo = c * FF_CHUNK
        return _dot(h, wi_ref[:, lo:lo + FF_CHUNK]), _dot(h, wi_ref[:, D_FF + lo:D_FF + lo + FF_CHUNK])

    def chunk(c):
        if c == 0:
            st["ahead"] = gate_up(0)
        gate, up = st["ahead"]
        if c + 1 < nchunks:
            st["ahead"] = gate_up(c + 1)
        act = (_silu(gate) * up).astype(BF16)
        part = _dot(act, wo_ref[c * FF_CHUNK:(c + 1) * FF_CHUNK, :])
        out["ffn"] = part if c == 0 else out["ffn"] + part

    return [(1, functools.partial(chunk, c)) for c in range(nchunks)]


def _swiglu_ffn(h, wi_ref, wo_ref):
    out = {}
    _run_interleaved(_swiglu_stages(h, wi_ref, wo_ref, out))
    return out["ffn"]


def _log2_forget(fpre, lb_ref):
    a0 = lb_ref[0:1, :]
    a1 = lb_ref[1:2, :]
    m = jnp.maximum(a0, a1)
    e0 = jnp.exp(a0 - m)
    e1 = jnp.exp(a1 - m)
    lb = e0 / (e0 + e1)
    return jnp.log(lb + (1.0 - lb) * jax.nn.sigmoid(fpre)) * (1.0 / math.log(2.0))


def _head_mean_square(x, seg_ref):
    seg = seg_ref[...]
    outs = []
    for j in range(x.shape[1] // V7X_LANES):
        xs = x[:, j * V7X_LANES:(j + 1) * V7X_LANES]
        sq = xs * xs
        hi = sq.astype(BF16)
        lo = (sq - hi.astype(F32)).astype(BF16)
        outs.append(_dot(jnp.concatenate([hi, lo], axis=1), seg))
    return outs[0] if len(outs) == 1 else jnp.concatenate(outs, axis=1)


def _rope(x, cos, sin_signed):
    lane = lax.broadcasted_iota(jnp.int32, cos.shape, 1)
    first_half = (lane & (AT_HEAD_DIM // 2)) == 0
    outs = []
    for j in range(x.shape[1] // V7X_LANES):
        xs = x[:, j * V7X_LANES:(j + 1) * V7X_LANES]
        partner = jnp.where(first_half,
                            pltpu.roll(xs, V7X_LANES - AT_HEAD_DIM // 2, axis=1),
                            pltpu.roll(xs, AT_HEAD_DIM // 2, axis=1))
        outs.append(xs * cos + partner * sin_signed)
    return outs[0] if len(outs) == 1 else jnp.concatenate(outs, axis=1)


def _qk_norm_rope(x, mean_square, gain, cos, sin_signed):
    return _rope(x * lax.rsqrt(mean_square + EPS) * gain, cos, sin_signed)


def _ffn_in_kernel(x_ref, n1_ref, wi_ref, wo_ref, n2_ref, win_ref, lbf_ref, lbb_ref, qn_ref, kn_ref,
                   cos_ref, sin_ref, seg_ref,
                   x1_ref, q_ref, lff_ref, lfb_ref, v_ref, g_ref, aq_ref, akv_ref):
    def half_stages(rows):
        st = {}

        def start():
            st["x"] = x_ref[rows, :]
            st["h"] = _rmsnorm(st["x"], n1_ref[...]).astype(BF16)

        def ffn_chunk(c):
            if c == 0:
                st["chunks"] = _swiglu_stages(st["h"], wi_ref, wo_ref, st)
            st["chunks"][c][1]()

        def residual():
            x1 = st["x"] + 0.5 * st["ffn"]
            x1_ref[rows, :] = x1
            st["h2"] = _rmsnorm(x1, n2_ref[...]).astype(BF16)

        def proj(lo, width):
            return _dot(st["h2"], win_ref[:, lo:lo + width])

        def attn_cols():
            st["aq"] = proj(COL_AQ, AT_WIDTH)
            st["akv"] = proj(COL_AKV, 2 * AT_KV_WIDTH)

        def forget_cols():
            lff_ref[rows, :] = _log2_forget(proj(COL_HF_FWD, HG_WIDTH), lbf_ref)
            lfb_ref[rows, :] = _log2_forget(proj(COL_HF_BWD, HG_WIDTH), lbb_ref)
            st["aq_ms"] = _head_mean_square(st["aq"], seg_ref)
            st["ak_ms"] = _head_mean_square(st["akv"][:, :AT_KV_WIDTH], seg_ref)

        def query_gate_cols():
            q_ref[rows, :] = _silu(proj(COL_HQ, HG_WIDTH))
            g_ref[rows, :] = _silu(proj(COL_HG, HG_WIDTH))

        def value_col():
            v_ref[rows, :] = proj(COL_HI, HG_WIDTH).astype(BF16)

        def rotary():
            cos = cos_ref[rows, :]
            sin = sin_ref[rows, :]
            aq = _qk_norm_rope(st["aq"], st["aq_ms"], qn_ref[...], cos, sin) * (1.0 / math.sqrt(AT_HEAD_DIM))
            aq_ref[rows, :] = aq.astype(BF16)
            akv = st["akv"]
            ak = _qk_norm_rope(akv[:, :AT_KV_WIDTH], st["ak_ms"], kn_ref[...], cos, sin)
            av = akv[:, AT_KV_WIDTH:]
            akv_ref[rows, :] = jnp.concatenate(
                [ak, pltpu.roll(ak, AT_HEAD_DIM, axis=1), av, pltpu.roll(av, AT_HEAD_DIM, axis=1)],
                axis=1).astype(BF16)

        return ([(0.1, start)] + [(3, functools.partial(ffn_chunk, c)) for c in range(D_FF // FF_CHUNK)]
                + [(0.1, residual), (3, attn_cols), (5, forget_cols), (0.1, rotary), (4, query_gate_cols),
                   (2, value_col)])

    tm = x_ref.shape[0]
    nparts = FFN_IN_SLICES
    parts = [half_stages(slice(i * tm // nparts, (i + 1) * tm // nparts)) for i in range(nparts)]
    total = sum(weight for weight, _ in parts[0])
    idle = lambda: None
    tiny = 1e-6
    _run_interleaved(*[[(tiny + i * total / nparts, idle)] + part + [(tiny + (nparts - 1 - i) * total / nparts, idle)]
                       for i, part in enumerate(parts)])


SUBLANES = 8
GROUP_LEVELS = 3


def _cumsum_rows(lf, tri_ref):
    hi = lf.astype(BF16)
    lo = (lf - hi.astype(F32)).astype(BF16)
    tri = tri_ref[...]
    return _dot(tri, hi) + _dot(tri, lo)


def _groups(a):
    return [a[g * SUBLANES:(g + 1) * SUBLANES, :] for g in range(a.shape[0] // SUBLANES)]


def _is_target_group(g, level, reverse):
    bit = (g >> (level - GROUP_LEVELS)) & 1
    return bit == (0 if reverse else 1)


def _gla_intra(q, lf, x, lvl, reverse):
    n = q.shape[0]
    ngroups = n // SUBLANES
    f = jnp.exp2(lf)
    k = 1.0 - f
    row = lax.broadcasted_iota(jnp.int32, q.shape, 0)
    lvl_g = _groups(lvl)
    a_g = [None] * ngroups

    def place(g, level, p_rows):
        keep = jnp.zeros_like(p_rows) if a_g[g] is None else a_g[g]
        a_g[g] = jnp.where(lvl_g[g] == level, p_rows, keep)

    for level in range(GROUP_LEVELS, SCAN_LEVELS):
        half = 1 << level
        q_rows, k_rows, targets = [], [], []
        for b in range(0, n, 2 * half):
            first, second = slice(b, b + half), slice(b + half, b + 2 * half)
            src, tgt = (second, first) if reverse else (first, second)
            r = b + half if reverse else b + half - 1
            x_r = x[r:r + 1, :]
            q_rows.append(q[tgt] * jnp.exp2(x[tgt] - x_r))
            k_src = k[src] * jnp.exp2(x_r - x[src])
            k_rows += [jnp.zeros_like(k_src), k_src] if reverse else [k_src, jnp.zeros_like(k_src)]
            targets += [g for g in range(b // SUBLANES, (b + 2 * half) // SUBLANES)
                        if _is_target_group(g, level, reverse)]
        p = _dot_nt(jnp.concatenate(q_rows, axis=0).astype(BF16), jnp.concatenate(k_rows, axis=0).astype(BF16))
        for i, g in enumerate(targets):
            place(g, level, p[i * SUBLANES:(i + 1) * SUBLANES, :])

    up = pltpu.roll(lf, n - 1, axis=0)
    down = pltpu.roll(lf, 1, axis=0)
    pos4 = row & 3
    if reverse:
        e1 = jnp.exp2(jnp.where(pos4 == 0, lf + up, jnp.where(pos4 == 1, lf, jnp.where(pos4 == 2, 0.0, down))))
        e0 = jnp.where((row & 1) == 0, f, 1.0)
        r2 = SUBLANES // 2
    else:
        e1 = jnp.exp2(jnp.where(pos4 == 0, up, jnp.where(pos4 == 1, 0.0, jnp.where(pos4 == 2, lf, lf + down))))
        e0 = jnp.where((row & 1) == 1, f, 1.0)
        r2 = SUBLANES // 2 - 1
    x_r2 = jnp.concatenate([jnp.broadcast_to(xg[r2:r2 + 1, :], xg.shape) for xg in _groups(x)], axis=0)
    e2 = jnp.exp2(-jnp.abs(x - x_r2))
    for level, e in enumerate((e0, e1, e2)):
        p = _dot_nt((q * e).astype(BF16), (k * e).astype(BF16))
        for g in range(ngroups):
            place(g, level, p[g * SUBLANES:(g + 1) * SUBLANES, :])

    a = jnp.concatenate(a_g, axis=0).astype(BF16)
    diag = jnp.sum(q * k, axis=-1, keepdims=True)
    x_end = x[0:1, :] if reverse else x[n - 1:n, :]
    q_in = (q * jnp.exp2(x)).astype(BF16)
    k_end = (k * jnp.exp2(x_end - x)).astype(BF16)
    return a, diag, q_in, k_end, jnp.exp2(x_end)


def _gla_inter(intra, v, state_ref):
    a, diag, q_in, k_end, decay = intra
    state = state_ref[...]
    o = _dot_nt(q_in, state.astype(BF16)) + _dot(a, v) + diag * v.astype(F32)
    state_ref[...] = state * decay + _dot_tn(v, k_end)
    return o


def _scan_kernel(lvl_f_ref, lvl_b_ref, tri_f_ref, tri_b_ref, qf_ref, lff_ref, vf_ref, qb_ref, lfb_ref, vb_ref,
                 of_ref, ob_ref, state_ref):
    @pl.when(pl.program_id(1) == 0)
    def _():
        state_ref[...] = jnp.zeros(state_ref.shape, F32)

    directions = ((qf_ref, lff_ref, vf_ref, of_ref, lvl_f_ref, tri_f_ref, False),
                  (qb_ref, lfb_ref, vb_ref, ob_ref, lvl_b_ref, tri_b_ref, True))
    heads = [(d, h) for d in range(2) for h in range(HG_HEADS)]
    cols = lambda h: slice(h * HG_DK, (h + 1) * HG_DK)
    nsub = qf_ref.shape[0] // SCAN_CHUNK
    chunk_rows = [(slice(j * SCAN_CHUNK, (j + 1) * SCAN_CHUNK),
                   slice((nsub - 1 - j) * SCAN_CHUNK, (nsub - j) * SCAN_CHUNK)) for j in range(nsub)]
    def intra(rows, cum, d, h):
        q_ref, lf_ref, _, _, lvl_ref, _, reverse = directions[d]
        return _gla_intra(q_ref[rows[d], cols(h)], lf_ref[rows[d], cols(h)], cum[d][:, cols(h)], lvl_ref[...],
                          reverse)

    def inter(rows, part, d, h):
        v_ref, o_ref = directions[d][2], directions[d][3]
        o_ref[rows[d], cols(h)] = _gla_inter(part, v_ref[rows[d], cols(h)], state_ref.at[d * HG_HEADS + h])

    pending = None
    for rows in chunk_rows:
        cum = [_cumsum_rows(lf_ref[rows[d], :], tri_ref)
               for d, (_, lf_ref, _, _, _, tri_ref, _) in enumerate(directions)]
        parts = []
        for i, (d, h) in enumerate(heads):
            parts.append(intra(rows, cum, d, h))
            if pending is not None:
                inter(pending[0], pending[1][i], d, h)
        pending = (rows, parts)
    for i, (d, h) in enumerate(heads):
        inter(pending[0], pending[1][i], d, h)


ATTN_LAG = 4


def _attn_stages(sink_ref, band_ref, q_ref, kvp_ref, kvc_ref, kvn_ref, o_ref, first_in_row, last_in_row):
    blk = ATT_BLOCK
    nsub = q_ref.shape[0] // blk
    st = {}

    def swapped(hd):
        return int(hd % 2 != hd // AT_GROUP)

    def prepare():
        neg_inf = jnp.float32(-jnp.inf)
        st["band_prev"] = band_ref[:, 0:blk]
        st["band_next"] = band_ref[:, blk:2 * blk]
        st["no_prev"] = jnp.where(first_in_row, neg_inf, 0.0)
        st["no_next"] = jnp.where(last_in_row, neg_inf, 0.0)
        lane = lax.broadcasted_iota(jnp.int32, (blk, V7X_LANES), 1)
        st["low"] = lane < AT_HEAD_DIM

        def kv_cat(col):
            sl = slice(col * V7X_LANES, (col + 1) * V7X_LANES)
            return jnp.concatenate([kvp_ref[:, sl], kvc_ref[:, sl], kvn_ref[:, sl]], axis=0)

        st["keys"] = (kv_cat(0), kv_cat(1))
        st["vals"] = (kv_cat(2), kv_cat(3))

    stacks = [[hd for hd in range(AT_HEADS) if swapped(hd) == which] for which in range(2)]

    def scores(j):
        out = [None] * AT_HEADS
        for which, heads in enumerate(stacks):
            qs = []
            for hd in heads:
                q_pair = q_ref[j * blk:(j + 1) * blk, (hd // 2) * V7X_LANES:(hd // 2 + 1) * V7X_LANES]
                qs.append(jnp.where(st["low"] if hd % 2 == 0 else ~st["low"], q_pair, jnp.zeros_like(q_pair)))
            s = _dot_nt(jnp.concatenate(qs, axis=0), st["keys"][which][j * blk:(j + 3) * blk])
            for n, hd in enumerate(heads):
                out[hd] = s[n * blk:(n + 1) * blk, :]
        st["scores", j] = out

    def softmax(j, pair):
        bias_prev = st["band_prev"] + st["no_prev"] if j == 0 else st["band_prev"]
        bias_next = st["band_next"] + st["no_next"] if j == nsub - 1 else st["band_next"]
        for side in range(2):
            hd = 2 * pair + side
            s = st["scores", j][hd]
            s_prev = s[:, 0:blk] + bias_prev
            s_cur = s[:, blk:2 * blk]
            s_next = s[:, 2 * blk:3 * blk] + bias_next
            sink = sink_ref[hd]
            m = jnp.max(jnp.maximum(jnp.maximum(s_prev, s_cur), s_next), axis=-1, keepdims=True)
            m = jnp.maximum(m, sink)
            p_prev = jnp.exp(s_prev - m)
            p_cur = jnp.exp(s_cur - m)
            p_next = jnp.exp(s_next - m)
            denom = jnp.sum(p_prev + p_cur + p_next, axis=-1, keepdims=True) + jnp.exp(sink - m)
            st["p", j, hd] = (jnp.concatenate([p_prev, p_cur, p_next], axis=1).astype(BF16), denom)

    def weighted_values(j):
        res = [None] * AT_HEADS
        for which, heads in enumerate(stacks):
            p = jnp.concatenate([st["p", j, hd][0] for hd in heads], axis=0)
            r = _dot(p, st["vals"][which][j * blk:(j + 3) * blk])
            for n, hd in enumerate(heads):
                res[hd] = r[n * blk:(n + 1) * blk, :] / st["p", j, hd][1]
        for pair in range(AT_HEADS // 2):
            o_ref[j * blk:(j + 1) * blk, pair * V7X_LANES:(pair + 1) * V7X_LANES] = (
                jnp.where(st["low"], res[2 * pair], res[2 * pair + 1]).astype(BF16))

    npairs = AT_HEADS // 2
    work = [(j, pair) for j in range(nsub) for pair in range(npairs)]
    stages = [(1, prepare)]
    for i in range(len(work) + ATTN_LAG):
        fns, weight = [], 0
        done = i - ATTN_LAG
        if done >= 0 and work[done][1] == npairs - 1:
            fns.append(functools.partial(weighted_values, work[done][0]))
            weight += 4
        if i < len(work):
            if work[i][1] == 0:
                fns.append(functools.partial(scores, work[i][0]))
            fns.append(functools.partial(softmax, *work[i]))
            weight += 4
        if fns:
            stages.append((weight, lambda fns=fns: [fn() for fn in fns]))
    return stages


def _ffn_out_kernel(x1_ref, of_ref, ob_ref, g_ref, onorm_ref, wout_ref, n3_ref, wi_ref, wo_ref,
                    sink_ref, band_ref, q0_ref, kvp0_ref, kvc0_ref, kvn0_ref, q_ref, kvp_ref, kvc_ref, kvn_ref,
                    y_ref, oat_ref, *, tiles_per_seq):
    s = pl.program_id(0)
    nxt = jnp.minimum(s + 1, pl.num_programs(0) - 1)
    pos = lax.rem(nxt, tiles_per_seq)

    @pl.when(s == 0)
    def _():
        _run_interleaved(_attn_stages(sink_ref, band_ref, q0_ref, kvp0_ref, kvc0_ref, kvn0_ref, oat_ref.at[0],
                                      True, tiles_per_seq == 1))

    slot = lax.rem(s, 2)
    st = {}

    def mix():
        o = of_ref[...] + ob_ref[...]
        g = g_ref[...]
        onorm = onorm_ref[...]
        mixed = []
        for h in range(HG_HEADS):
            sl = slice(h * HG_DK, (h + 1) * HG_DK)
            mixed.append((_rmsnorm(o[:, sl], onorm) * g[:, sl]).astype(BF16))
        mixed.append(oat_ref[slot])
        st["x2"] = x1_ref[...] + _dot(jnp.concatenate(mixed, axis=1), wout_ref[...])
        st["h2"] = _rmsnorm(st["x2"], n3_ref[...]).astype(BF16)

    def ffn_chunk(c):
        if c == 0:
            st["chunks"] = _swiglu_stages(st["h2"], wi_ref, wo_ref, st)
        st["chunks"][c][1]()

    def finish():
        y_ref[...] = st["x2"] + 0.5 * st["ffn"]

    dense = ([(1, mix)] + [(3, functools.partial(ffn_chunk, c)) for c in range(D_FF // FF_CHUNK)] + [(1, finish)])
    _run_interleaved(dense, _attn_stages(sink_ref, band_ref, q_ref, kvp_ref, kvc_ref, kvn_ref, oat_ref.at[1 - slot],
                                         pos == 0, pos == tiles_per_seq - 1))


def _resident(shape):
    return pl.BlockSpec(shape, lambda *_: (0,) * len(shape), pipeline_mode=pl.Buffered(1))


def _params(semantics):
    return pltpu.CompilerParams(dimension_semantics=semantics, vmem_limit_bytes=V7X_VMEM_LIMIT)


def _scan_level_table(reverse):
    t = np.arange(SCAN_CHUNK)[:, None]
    s = np.arange(SCAN_CHUNK)[None, :]
    x = t ^ s
    lvl = np.where(x > 0, np.floor(np.log2(np.maximum(x, 1))).astype(np.int32), -1)
    valid = (s > t) if reverse else (s < t)
    return jnp.asarray(np.where(valid, lvl, -1).astype(np.int32))


def _band_bias():
    r = np.arange(ATT_BLOCK)[:, None]
    c = np.arange(ATT_BLOCK)[None, :]
    prev = np.where(c >= r, 0.0, -np.inf)
    nxt = np.where(c <= r, 0.0, -np.inf)
    return jnp.asarray(np.concatenate([prev, nxt], axis=1).astype(np.float32))


def _rope_tables(seq):
    pos = jnp.arange(seq, dtype=F32)
    inv_freq = ROPE_THETA ** (-jnp.arange(0, AT_HEAD_DIM, 2, dtype=F32) / AT_HEAD_DIM)
    ang = pos[:, None] * inv_freq[None, :]
    cos, sin = jnp.cos(ang), jnp.sin(ang)
    reps = V7X_LANES // AT_HEAD_DIM
    return jnp.tile(cos, (1, 2 * reps)), jnp.tile(jnp.concatenate([-sin, sin], axis=1), (1, reps))


def _layer(x, w):
    batch, seq, _ = x.shape
    tokens = batch * seq
    tm = TOKEN_TILE
    x2d = x.reshape(tokens, D_MODEL)
    tiles_per_seq = seq // tm

    row = lambda width: pl.BlockSpec((tm, width), lambda i: (i, 0))
    pos = lambda width: pl.BlockSpec((tm, width), lambda i: (i % tiles_per_seq, 0))
    sds = lambda width, dt: jax.ShapeDtypeStruct((tokens, width), dt)

    x1, q, lff, lfb, v, g, aq, akv = pl.pallas_call(
        _ffn_in_kernel,
        name="ffn_in",
        grid=(tokens // tm,),
        in_specs=[row(D_MODEL), _resident((1, D_MODEL)), _resident((D_MODEL, 2 * D_FF)),
                  _resident((D_FF, D_MODEL)), _resident((1, D_MODEL)), _resident((D_MODEL, N_IN)),
                  _resident((2, HG_WIDTH)), _resident((2, HG_WIDTH)), _resident((1, AT_WIDTH)),
                  _resident((1, AT_KV_WIDTH)), pos(V7X_LANES), pos(V7X_LANES),
                  _resident((2 * V7X_LANES, V7X_LANES))],
        out_specs=[row(D_MODEL), row(HG_WIDTH), row(HG_WIDTH), row(HG_WIDTH), row(HG_WIDTH), row(HG_WIDTH),
                   row(AT_WIDTH), row(4 * AT_KV_WIDTH)],
        out_shape=[sds(D_MODEL, F32), sds(HG_WIDTH, F32), sds(HG_WIDTH, F32), sds(HG_WIDTH, F32),
                   sds(HG_WIDTH, BF16), sds(HG_WIDTH, F32), sds(AT_WIDTH, BF16), sds(4 * AT_KV_WIDTH, BF16)],
        compiler_params=_params(("parallel",)),
    )(x2d, w["ffn1_norm"], w["ffn1_wi"], w["ffn1_wo"], w["mix_norm"], w["w_in"], w["hg_lb_fwd"],
      w["hg_lb_bwd"], w["q_norm"], w["k_norm"], w["cos"], w["sin"], w["seg"])

    c = SCAN_CHUNK
    nc = seq // SCAN_BLOCK
    fwd = lambda b, i: (b * nc + i, 0)
    bwd = lambda b, i: (b * nc + nc - 1 - i, 0)
    chunk = lambda imap: pl.BlockSpec((SCAN_BLOCK, HG_WIDTH), imap)
    o_f, o_b = pl.pallas_call(
        _scan_kernel,
        name="hgrn_scan",
        grid=(batch, nc),
        in_specs=[_resident((c, c)), _resident((c, c)), _resident((c, c)), _resident((c, c)),
                  chunk(fwd), chunk(fwd), chunk(fwd), chunk(bwd), chunk(bwd), chunk(bwd)],
        out_specs=[chunk(fwd), chunk(bwd)],
        out_shape=[sds(HG_WIDTH, F32), sds(HG_WIDTH, F32)],
        scratch_shapes=[pltpu.VMEM((2 * HG_HEADS, HG_DK, HG_DK), F32)],
        compiler_params=_params(("parallel", "arbitrary")),
    )(w["lvl_fwd"], w["lvl_bwd"], w["tri_fwd"], w["tri_bwd"], q, lff, v, q, lfb, v)

    blk = ATT_BLOCK
    per_tile = tm // blk
    ntiles = tokens // tm
    nblocks = tokens // blk
    ahead = lambda i: jnp.minimum(i + 1, ntiles - 1)
    kv_width = 4 * AT_KV_WIDTH
    before = lambda t: jnp.maximum(per_tile * t - 1, 0)
    after = lambda t: jnp.minimum(per_tile * t + per_tile, nblocks - 1)
    y = pl.pallas_call(
        functools.partial(_ffn_out_kernel, tiles_per_seq=tiles_per_seq),
        name="ffn_out",
        grid=(ntiles,),
        in_specs=[row(D_MODEL), row(HG_WIDTH), row(HG_WIDTH), row(HG_WIDTH),
                  _resident((1, HG_DK)), _resident((D_MODEL, D_MODEL)), _resident((1, D_MODEL)),
                  _resident((D_MODEL, 2 * D_FF)), _resident((D_FF, D_MODEL)),
                  pl.BlockSpec(memory_space=pltpu.SMEM), _resident((blk, 2 * blk)),
                  pl.BlockSpec((tm, AT_WIDTH), lambda i: (0, 0)),
                  pl.BlockSpec((blk, kv_width), lambda i: (0, 0)),
                  pl.BlockSpec((tm, kv_width), lambda i: (0, 0)),
                  pl.BlockSpec((blk, kv_width), lambda i: (min(per_tile, nblocks - 1), 0)),
                  pl.BlockSpec((tm, AT_WIDTH), lambda i: (ahead(i), 0)),
                  pl.BlockSpec((blk, kv_width), lambda i: (before(ahead(i)), 0)),
                  pl.BlockSpec((tm, kv_width), lambda i: (ahead(i), 0)),
                  pl.BlockSpec((blk, kv_width), lambda i: (after(ahead(i)), 0))],
        out_specs=row(D_MODEL),
        out_shape=sds(D_MODEL, F32),
        scratch_shapes=[pltpu.VMEM((2, tm, AT_WIDTH), BF16)],
        compiler_params=_params(("arbitrary",)),
    )(x1, o_f, o_b, g, w["hg_out_norm"], w["w_out"], w["ffn2_norm"], w["ffn2_wi"], w["ffn2_wo"],
      w["attn_sink"], w["band"], aq, akv, akv, akv, aq, akv, akv, akv)
    return y.reshape(batch, seq, D_MODEL)


def kernel(x_prompt, x_sample, ffn1_norm, ffn1_wi, ffn1_wo, mix_norm, w_in, hg_lb_fwd, hg_lb_bwd, hg_out_norm,
           q_norm, k_norm, attn_sink, w_out, ffn2_norm, ffn2_wi, ffn2_wo):
    seq = x_prompt.shape[1]
    assert x_sample.shape[1] == seq and seq % TOKEN_TILE == 0 and seq % SCAN_BLOCK == 0
    cos, sin = _rope_tables(seq)
    heads_per_group = V7X_LANES // AT_HEAD_DIM
    seg = jnp.kron(jnp.eye(heads_per_group, dtype=F32), jnp.full((AT_HEAD_DIM, AT_HEAD_DIM), 1.0 / AT_HEAD_DIM, F32))
    seg = jnp.concatenate([seg, seg], axis=0)
    w = {
        "ffn1_norm": ffn1_norm[0][None, :], "ffn1_wi": ffn1_wi[0].astype(BF16), "ffn1_wo": ffn1_wo[0].astype(BF16),
        "mix_norm": mix_norm[0][None, :], "w_in": w_in[0].astype(BF16),
        "hg_lb_fwd": hg_lb_fwd, "hg_lb_bwd": hg_lb_bwd, "hg_out_norm": hg_out_norm[0][None, :],
        "q_norm": jnp.tile(q_norm[0], AT_HEADS)[None, :], "k_norm": jnp.tile(k_norm[0], AT_KV_HEADS)[None, :],
        "attn_sink": attn_sink[0], "w_out": w_out[0].astype(BF16),
        "ffn2_norm": ffn2_norm[0][None, :], "ffn2_wi": ffn2_wi[0].astype(BF16), "ffn2_wo": ffn2_wo[0].astype(BF16),
        "cos": cos, "sin": sin, "seg": seg.astype(BF16),
        "lvl_fwd": _scan_level_table(False), "lvl_bwd": _scan_level_table(True), "band": _band_bias(),
        "tri_fwd": jnp.asarray(np.tril(np.ones((SCAN_CHUNK, SCAN_CHUNK), np.float32)), BF16),
        "tri_bwd": jnp.asarray(np.triu(np.ones((SCAN_CHUNK, SCAN_CHUNK), np.float32)), BF16),
    }
    return (_layer(x_prompt, w), _layer(x_sample, w))
```
